```python
import math
import numpy as np
import jax
import jax.numpy as jnp
from jax import lax

D_MODEL = 1024
BATCH = 4
SEQ = 8192
DEPTH = 1

CHUNK = 64
EPS = 1e-6
ADA_PARTS = 6

M_HEADS = 4
M_HEAD_DIM = 256
M_WIDTH = M_HEADS * M_HEAD_DIM
CONV_WIDTH = 4
F_BIAS_LO = 3.0
F_BIAS_HI = 6.0

D_HEADS = 4
D_QK_DIM = 128
D_V_DIM = 2 * D_QK_DIM
D_QK_WIDTH = D_HEADS * 2 * D_QK_DIM
D_V_WIDTH = D_HEADS * D_V_DIM
Q_BLOCK = 128

IN_WIDTH = 2 * M_WIDTH + M_WIDTH + M_WIDTH + 2 * M_HEADS + 2 * D_QK_WIDTH + D_V_WIDTH + 2 * D_MODEL

P_HEADS = 8
N_KEYS = 128
N_EXPERTS = N_KEYS * N_KEYS
P_TOPK = 16
P_QUERY_DIM = 256
P_HALF = P_QUERY_DIM // 2
P_TOKEN_BLOCK = 128

kernel_name = "hybrid_mlstm_diffattn_peer_block"


def rms_norm(x, gain):
    xf = x.astype(jnp.float32)
    y = xf * lax.rsqrt(jnp.mean(xf * xf, axis=-1, keepdims=True) + EPS)
    return (y * gain.astype(jnp.float32)).astype(x.dtype)


def lambda_init(layer):
    return 0.8 - 0.6 * math.exp(-0.3 * layer)


def causal_depthwise_conv(x, w, b):
    y = lax.conv_general_dilated(
        x, w[:, None, :].astype(x.dtype), window_strides=(1,),
        padding=[(CONV_WIDTH - 1, 0)], dimension_numbers=("NWC", "WIO", "NWC"),
        feature_group_count=x.shape[-1])
    return y + b.astype(x.dtype)


def mlstm_chunkwise(q, k, v, log_i, log_f):
    B, H, S, Dh = q.shape
    nc = S // CHUNK

    def to_chunks(t):
        return jnp.moveaxis(t.reshape(B, H, nc, CHUNK, *t.shape[3:]), 2, 0)

    causal = jnp.tril(jnp.ones((CHUNK, CHUNK), dtype=bool))

    def step(carry, inp):
        C, n, m = carry
        qb, kb, vb, lib, lfb = inp
        b = jnp.cumsum(lfb, axis=-1)
        a = b + m[..., None]
        d = jnp.where(causal, b[..., :, None] - b[..., None, :] + lib[..., None, :], -jnp.inf)
        m_t = jnp.maximum(a, jnp.max(d, axis=-1))
        w_inter = jnp.exp(a - m_t)
        s = jnp.einsum("bhtd,bhsd->bhts", qb, kb) * jnp.exp(d - m_t[..., None])
        num = jnp.einsum("bhts,bhsd->bhtd", s, vb) + w_inter[..., None] * jnp.einsum("bhvk,bhtk->bhtv", C, qb)
        den = jnp.sum(s, axis=-1) + w_inter * jnp.einsum("bhk,bhtk->bht", n, qb)
        h = num / jnp.maximum(jnp.abs(den), jnp.exp(-m_t))[..., None]
        m_new = m_t[..., -1]
        g_prev = jnp.exp(b[..., -1] + m - m_new)
        g_s = jnp.exp(b[..., -1:] - b + lib - m_new[..., None])
        C_new = g_prev[..., None, None] * C + jnp.einsum("bhs,bhsv,bhsk->bhvk", g_s, vb, kb)
        n_new = g_prev[..., None] * n + jnp.einsum("bhs,bhsk->bhk", g_s, kb)
        return (C_new, n_new, m_new), h

    init = (jnp.zeros((B, H, Dh, Dh), jnp.float32),
            jnp.zeros((B, H, Dh), jnp.float32),
            jnp.zeros((B, H), jnp.float32))
    xs = (to_chunks(q), to_chunks(k), to_chunks(v), to_chunks(log_i), to_chunks(log_f))
    _, hs = lax.scan(step, init, xs)
    return jnp.moveaxis(hs, 0, 2).reshape(B, H, S, Dh)


def diff_attention(q, k, v, lam):
    B, S, H, _, dqk = q.shape
    nqb = S // Q_BLOCK
    scale = dqk ** -0.5
    key_chunk = jnp.arange(S) // CHUNK
    q_blocks = jnp.moveaxis(q.reshape(B, nqb, Q_BLOCK, H, 2, dqk), 1, 0)

    def block(args):
        qb, blk = args
        q_chunk = (blk * Q_BLOCK + jnp.arange(Q_BLOCK)) // CHUNK
        mask = key_chunk[None, :] <= q_chunk[:, None]
        s = jnp.einsum("bqhcd,bkhcd->bhcqk", qb, k).astype(jnp.float32) * scale
        p = jax.nn.softmax(jnp.where(mask, s, -jnp.inf), axis=-1)
        a = p[:, :, 0] - lam * p[:, :, 1]
        return jnp.einsum("bhqk,bkhd->bqhd", a.astype(v.dtype), v)

    out = lax.map(block, (q_blocks, jnp.arange(nqb)))
    return jnp.moveaxis(out, 0, 1).reshape(B, S, H, v.shape[-1])


def hybrid_mixer(h, lam_init, w_in, b_if, conv_w, conv_b, m_head_gain, lam_q1, lam_k1, lam_q2, lam_k2,
                 d_head_gain, w_br_m, w_br_d, w_out):
    B, S, _ = h.shape
    proj = h @ w_in
    sizes = (2 * M_WIDTH, M_WIDTH, M_WIDTH, 2 * M_HEADS, D_QK_WIDTH, D_QK_WIDTH, D_V_WIDTH, D_MODEL, D_MODEL)
    m_qk, m_v, m_o, m_if, d_q, d_k, d_v, g_m, g_d = jnp.split(proj, np.cumsum(sizes)[:-1].tolist(), axis=-1)

    m_qk = jax.nn.silu(causal_depthwise_conv(m_qk, conv_w, conv_b))
    mq, mk = jnp.split(m_qk, 2, axis=-1)

    def heads(t):
        return t.reshape(B, S, M_HEADS, M_HEAD_DIM).transpose(0, 2, 1, 3).astype(jnp.float32)

    gate_pre = (m_if + b_if).astype(jnp.float32).transpose(0, 2, 1)
    log_i = gate_pre[:, :M_HEADS]
    log_f = jax.nn.log_sigmoid(gate_pre[:, M_HEADS:])
    cell = mlstm_chunkwise(heads(mq), heads(mk) * (M_HEAD_DIM ** -0.5), heads(m_v), log_i, log_f)
    cell = cell.transpose(0, 2, 1, 3).astype(h.dtype)
    y_m = rms_norm(cell, m_head_gain.reshape(M_HEADS, M_HEAD_DIM)).reshape(B, S, M_WIDTH) * jax.nn.sigmoid(m_o)

    f32 = jnp.float32
    lam = (jnp.exp(jnp.sum(lam_q1.astype(f32) * lam_k1.astype(f32)))
           - jnp.exp(jnp.sum(lam_q2.astype(f32) * lam_k2.astype(f32))) + lam_init)
    att = diff_attention(d_q.reshape(B, S, D_HEADS, 2, D_QK_DIM),
                         d_k.reshape(B, S, D_HEADS, 2, D_QK_DIM),
                         d_v.reshape(B, S, D_HEADS, D_V_DIM), lam)
    y_d = rms_norm(att, d_head_gain.reshape(D_HEADS, D_V_DIM)).reshape(B, S, D_V_WIDTH) * (1.0 - lam_init)

    merged = jax.nn.sigmoid(g_m) * (y_m @ w_br_m) + jax.nn.sigmoid(g_d) * (y_d @ w_br_d)
    return merged @ w_out


def peer_ffn(h, w_query, sub_keys, expert_u, expert_v):
    T, D = h.shape
    qry = (h @ w_query).reshape(T, P_HEADS, 2, P_HALF)
    scores = jnp.einsum("thcd,hcnd->thcn", qry, sub_keys).astype(jnp.float32)
    s_top, i_top = lax.top_k(scores, P_TOPK)
    cand_s = (s_top[:, :, 0, :, None] + s_top[:, :, 1, None, :]).reshape(T, P_HEADS, P_TOPK * P_TOPK)
    cand_i = (i_top[:, :, 0, :, None] * N_KEYS + i_top[:, :, 1, None, :]).reshape(T, P_HEADS, P_TOPK * P_TOPK)
    best_s, best_pos = lax.top_k(cand_s, P_TOPK)
    expert_idx = jnp.take_along_axis(cand_i, best_pos, axis=-1)
    gates = jax.nn.softmax(best_s, axis=-1).astype(h.dtype)
    nb = T // P_TOKEN_BLOCK

    def block(args):
        hb, eb, gb = args
        act = jax.nn.gelu(jnp.einsum("td,thkd->thk", hb, expert_u[eb]), approximate=False)
        return jnp.einsum("thk,thkd->td", act * gb, expert_v[eb])

    out = lax.map(block, (h.reshape(nb, P_TOKEN_BLOCK, D),
                          expert_idx.reshape(nb, P_TOKEN_BLOCK, P_HEADS, P_TOPK),
                          gates.reshape(nb, P_TOKEN_BLOCK, P_HEADS, P_TOPK)))
    return out.reshape(T, D)


def setup_inputs(seed: int = 0) -> dict:
    key = jax.random.key(seed)
    ks = jax.random.split(key, 26)
    f32 = jnp.float32

    def nrm(k, shape, scale):
        return jax.random.normal(k, shape, f32) * scale

    def gain(k, width):
        return 1.0 + nrm(k, (DEPTH, width), 0.05)

    f_bias = jnp.linspace(F_BIAS_LO, F_BIAS_HI, M_HEADS, dtype=f32)
    b_if = jnp.concatenate([nrm(ks[9], (DEPTH, M_HEADS), 0.1),
                            f_bias + nrm(ks[10], (DEPTH, M_HEADS), 0.1)], axis=-1)
    return {
        "x": nrm(ks[0], (BATCH, SEQ, D_MODEL), 1.0),
        "c": nrm(ks[1], (BATCH, D_MODEL), 1.0),
        "w_ada": nrm(ks[2], (DEPTH, D_MODEL, ADA_PARTS * D_MODEL), 0.5 * D_MODEL ** -0.5),
        "b_ada": nrm(ks[3], (DEPTH, ADA_PARTS * D_MODEL), 0.01),
        "g_pre_mix": gain(ks[4], D_MODEL),
        "g_post_mix": gain(ks[5], D_MODEL),
        "g_pre_ffn": gain(ks[6], D_MODEL),
        "g_post_ffn": gain(ks[7], D_MODEL),
        "w_in": nrm(ks[8], (DEPTH, D_MODEL, IN_WIDTH), D_MODEL ** -0.5),
        "b_if": b_if,
        "conv_w": nrm(ks[11], (DEPTH, CONV_WIDTH, 2 * M_WIDTH), CONV_WIDTH ** -0.5),
        "conv_b": nrm(ks[12], (DEPTH, 2 * M_WIDTH), 0.01),
        "m_head_gain": gain(ks[13], M_WIDTH),
        "lam_q1": nrm(ks[14], (DEPTH, D_QK_DIM), 0.1),
        "lam_k1": nrm(ks[15], (DEPTH, D_QK_DIM), 0.1),
        "lam_q2": nrm(ks[16], (DEPTH, D_QK_DIM), 0.1),
        "lam_k2": nrm(ks[17], (DEPTH, D_QK_DIM), 0.1),
        "d_head_gain": gain(ks[18], D_V_WIDTH),
        "w_br_m": nrm(ks[19], (DEPTH, M_WIDTH, D_MODEL), M_WIDTH ** -0.5),
        "w_br_d": nrm(ks[20], (DEPTH, D_V_WIDTH, D_MODEL), D_V_WIDTH ** -0.5),
        "w_out": nrm(ks[21], (DEPTH, D_MODEL, D_MODEL), D_MODEL ** -0.5),
        "w_query": nrm(ks[22], (DEPTH, D_MODEL, P_HEADS * P_QUERY_DIM), D_MODEL ** -0.5),
        "sub_keys": nrm(ks[23], (DEPTH, P_HEADS, 2, N_KEYS, P_HALF), P_HALF ** -0.5),
        "expert_u": nrm(ks[24], (DEPTH, N_EXPERTS, D_MODEL), D_MODEL ** -0.5),
        "expert_v": nrm(ks[25], (DEPTH, N_EXPERTS, D_MODEL), (P_HEADS * P_TOPK) ** -0.5),
    }


def reference(x, c, w_ada, b_ada, g_pre_mix, g_post_mix, g_pre_ffn, g_post_ffn, w_in, b_if, conv_w, conv_b,
              m_head_gain, lam_q1, lam_k1, lam_q2, lam_k2, d_head_gain, w_br_m, w_br_d, w_out,
              w_query, sub_keys, expert_u, expert_v):
    B, S, D = x.shape
    for l in range(DEPTH):
        ada = (jax.nn.silu(c) @ w_ada[l] + b_ada[l])[:, None, :]
        shift1, scale1, gate1, shift2, scale2, gate2 = jnp.split(ada, ADA_PARTS, axis=-1)
        h = rms_norm(x, g_pre_mix[l]) * (1.0 + scale1) + shift1
        y = hybrid_mixer(h, lambda_init(l), w_in[l], b_if[l], conv_w[l], conv_b[l], m_head_gain[l],
                         lam_q1[l], lam_k1[l], lam_q2[l], lam_k2[l], d_head_gain[l],
                         w_br_m[l], w_br_d[l], w_out[l])
        x = x + gate1 * rms_norm(y, g_post_mix[l])
        h = rms_norm(x, g_pre_ffn[l]) * (1.0 + scale2) + shift2
        y = peer_ffn(h.reshape(B * S, D), w_query[l], sub_keys[l], expert_u[l], expert_v[l]).reshape(B, S, D)
        x = x + gate2 * rms_norm(y, g_post_ffn[l])
    return x
```

```python
import functools
import math

import jax
import jax.numpy as jnp
from jax import lax
from jax.experimental import pallas as pl
from jax.experimental.pallas import tpu as pltpu

F32 = jnp.float32
BF16 = jnp.bfloat16
HIGHEST = lax.Precision.HIGHEST

EPS = 1e-6
ADA_PARTS = 6
CHUNK = 64
M_HEADS = 4
M_HEAD_DIM = 256
CONV_WIDTH = 4
D_HEADS = 4
D_QK_DIM = 128
D_V_DIM = 256
P_HEADS = 8
N_KEYS = 128
P_TOPK = 16
P_HALF = 128
LAM_INIT = 0.8 - 0.6 * math.exp(-0.3 * 0)

V7X_VMEM_LIMIT_BYTES = 56 * 1024 * 1024

NT_DIMS = (((1,), (1,)), ((), ()))
TN_DIMS = (((0,), (0,)), ((), ()))


def _cparams(semantics):
    return pltpu.CompilerParams(dimension_semantics=semantics, vmem_limit_bytes=V7X_VMEM_LIMIT_BYTES)


def _rms(x, gain):
    return x * lax.rsqrt(jnp.mean(x * x, axis=-1, keepdims=True) + EPS) * gain


def _ada_kernel(c_ref, w_ref, b_ref, o_ref):
    c = c_ref[...]
    sc = c * jax.nn.sigmoid(c)
    o_ref[...] = jnp.dot(sc.astype(BF16), w_ref[...].astype(BF16), preferred_element_type=F32) + b_ref[...]


def _ada(c8, w_ada, b_ada):
    d = c8.shape[1]
    n = w_ada.shape[1]
    tn = 1024
    return pl.pallas_call(
        _ada_kernel,
        grid=(n // tn,),
        in_specs=[pl.BlockSpec((8, d), lambda j: (0, 0)),
                  pl.BlockSpec((d, tn), lambda j: (0, j)),
                  pl.BlockSpec((1, tn), lambda j: (0, j))],
        out_specs=pl.BlockSpec((8, tn), lambda j: (0, j)),
        out_shape=jax.ShapeDtypeStruct((8, n), F32),
        compiler_params=_cparams(("arbitrary",)),
        name="ada",
    )(c8, w_ada, b_ada)


def _inproj_kernel(x_ref, g_ref, sc_ref, sh_ref, w_ref, wif_ref, bif_ref, o_ref, oif_ref, h_scr):
    @pl.when(pl.program_id(2) == 0)
    def _():
        h = _rms(x_ref[...], g_ref[...]) * (1.0 + sc_ref[...]) + sh_ref[...]
        hb = h.astype(BF16)
        h_scr[...] = hb
        oif_ref[...] = jnp.dot(hb, wif_ref[...], preferred_element_type=F32) + bif_ref[...]

    o_ref[...] = jnp.dot(h_scr[...], w_ref[...], preferred_element_type=F32).astype(o_ref.dtype)


def _inproj(x, gain, scale, shift, w_main, w_if, b_if, tm, tn):
    B, S, D = x.shape
    n = w_main.shape[1]
    return pl.pallas_call(
        _inproj_kernel,
        grid=(B, S // tm, n // tn),
        in_specs=[pl.BlockSpec((None, tm, D), lambda b, i, j: (b, i, 0)),
                  pl.BlockSpec((1, D), lambda b, i, j: (0, 0)),
                  pl.BlockSpec((None, 1, D), lambda b, i, j: (b, 0, 0)),
                  pl.BlockSpec((None, 1, D), lambda b, i, j: (b, 0, 0)),
                  pl.BlockSpec((D, tn), lambda b, i, j: (0, j)),
                  pl.BlockSpec((D, 128), lambda b, i, j: (0, 0)),
                  pl.BlockSpec((1, 128), lambda b, i, j: (0, 0))],
        out_specs=[pl.BlockSpec((None, tm, tn), lambda b, i, j: (b, i, j)),
                   pl.BlockSpec((None, tm, 128), lambda b, i, j: (b, i, 0))],
        out_shape=[jax.ShapeDtypeStruct((B, S, n), BF16),
                   jax.ShapeDtypeStruct((B, S, 128), F32)],
        scratch_shapes=[pltpu.VMEM((tm, D), BF16)],
        compiler_params=_cparams(("parallel", "parallel", "arbitrary")),
        name="inproj",
    )(x, gain, scale, shift, w_main, w_if, b_if)


def _mlstm_kernel(q_ref, k_ref, v_ref, mo_ref, cwq_ref, cwk_ref, cbq_ref, cbk_ref, gi_ref, gf_ref, gain_ref,
                  y_ref, ct_scr, n_scr, m_scr, qtail, ktail, xext, *, L):
    @pl.when(pl.program_id(2) == 0)
    def _():
        ct_scr[...] = jnp.zeros_like(ct_scr)
        n_scr[...] = jnp.zeros_like(n_scr)
        m_scr[...] = jnp.zeros_like(m_scr)
        qtail[...] = jnp.zeros_like(qtail)
        ktail[...] = jnp.zeros_like(ktail)

    def conv_silu(x_ref, tail, cw_ref, cb_ref):
        x = x_ref[...].astype(F32)
        xext[0:8, :] = tail[...]
        xext[8:, :] = x
        acc = jnp.broadcast_to(cb_ref[...], x.shape)
        for j in range(CONV_WIDTH):
            acc = acc + cw_ref[j:j + 1, :] * xext[5 + j:5 + j + L, :]
        tail[...] = x[L - 8:, :]
        return acc * jax.nn.sigmoid(acc)

    q = conv_silu(q_ref, qtail, cwq_ref, cbq_ref)
    k = conv_silu(k_ref, ktail, cwk_ref, cbk_ref) * (M_HEAD_DIM ** -0.5)
    v = v_ref[...]
    qb = q.astype(BF16)
    kb = k.astype(BF16)

    li_row = gi_ref[...]
    fp = gf_ref[...]
    lf_row = jnp.minimum(fp, 0.0) - jnp.log1p(jnp.exp(-jnp.abs(fp)))

    ti = lax.broadcasted_iota(jnp.int32, (L, L), 0)
    si = lax.broadcasted_iota(jnp.int32, (L, L), 1)
    causal = si <= ti
    tril = causal.astype(F32)
    eye = (si == ti).astype(F32)
    rid = lax.broadcasted_iota(jnp.int32, (8, L), 0)
    rows = jnp.where(rid == 0, lf_row, jnp.where(rid == 1, li_row, 0.0))
    b_row = lax.dot_general(rows, tril, NT_DIMS, precision=HIGHEST, preferred_element_type=F32)[0:1, :]
    cum_cols = lax.dot_general(tril, rows, NT_DIMS, precision=HIGHEST, preferred_element_type=F32)
    raw_cols = lax.dot_general(eye, rows, NT_DIMS, precision=HIGHEST, preferred_element_type=F32)
    b_col = cum_cols[:, 0:1]
    li_col = raw_cols[:, 1:2]

    m_prev = m_scr[0:1, 0:1]
    a_col = b_col + m_prev
    dmat = jnp.where(causal, b_col - b_row + li_row, -jnp.inf)
    m_t = jnp.maximum(a_col, jnp.max(dmat, axis=1, keepdims=True))
    p = jnp.exp(dmat - m_t)
    s = lax.dot_general(qb, kb, NT_DIMS, preferred_element_type=F32) * p
    w_inter = jnp.exp(a_col - m_t)
    ct = ct_scr[...]
    num = (jnp.dot(s.astype(BF16), v, preferred_element_type=F32)
           + w_inter * jnp.dot(qb, ct.astype(BF16), preferred_element_type=F32))
    n_row = n_scr[...]
    den = jnp.sum(s, axis=1, keepdims=True) + w_inter * jnp.sum(q * n_row, axis=1, keepdims=True)
    h = num / jnp.maximum(jnp.abs(den), jnp.exp(-m_t))

    m_new = m_t[L - 1:L, :]
    b_last = b_col[L - 1:L, :]
    g_prev = jnp.exp(b_last + m_prev - m_new)
    gs_col = jnp.exp(b_last - b_col + li_col - m_new)
    gv = (gs_col * v.astype(F32)).astype(BF16)
    ct_scr[...] = g_prev * ct + lax.dot_general(kb, gv, TN_DIMS, preferred_element_type=F32)
    n_scr[...] = g_prev * n_row + jnp.sum(gs_col * k, axis=0, keepdims=True)
    m_scr[...] = jnp.broadcast_to(m_new, m_scr.shape)

    y = _rms(h, gain_ref[...]) * jax.nn.sigmoid(mo_ref[...].astype(F32))
    y_ref[...] = y.astype(y_ref.dtype)


def _mlstm(proj, gates_rows, conv_w, conv_b, head_gain, L):
    B, S, _ = proj.shape
    H, Dh = M_HEADS, M_HEAD_DIM
    col = lambda off: (lambda b, h, c: (b, c, off + h))
    return pl.pallas_call(
        functools.partial(_mlstm_kernel, L=L),
        grid=(B, H, S // L),
        in_specs=[pl.BlockSpec((None, L, Dh), col(0)),
                  pl.BlockSpec((None, L, Dh), col(H)),
                  pl.BlockSpec((None, L, Dh), col(2 * H)),
                  pl.BlockSpec((None, L, Dh), col(3 * H)),
                  pl.BlockSpec((CONV_WIDTH, Dh), lambda b, h, c: (0, h)),
                  pl.BlockSpec((CONV_WIDTH, Dh), lambda b, h, c: (0, H + h)),
                  pl.BlockSpec((1, Dh), lambda b, h, c: (0, h)),
                  pl.BlockSpec((1, Dh), lambda b, h, c: (0, H + h)),
                  pl.BlockSpec((None, None, 1, L), lambda b, h, c: (b, h, 0, c)),
                  pl.BlockSpec((None, None, 1, L), lambda b, h, c: (b, H + h, 0, c)),
                  pl.BlockSpec((1, Dh), lambda b, h, c: (0, h))],
        out_specs=pl.BlockSpec((None, L, Dh), lambda b, h, c: (b, c, h)),
        out_shape=jax.ShapeDtypeStruct((B, S, H * Dh), BF16),
        scratch_shapes=[pltpu.VMEM((Dh, Dh), F32), pltpu.VMEM((1, Dh), F32), pltpu.VMEM((8, 128), F32),
                        pltpu.VMEM((8, Dh), F32), pltpu.VMEM((8, Dh), F32), pltpu.VMEM((L + 8, Dh), F32)],
        compiler_params=_cparams(("parallel", "parallel", "arbitrary")),
        name="mlstm",
    )(proj, proj, proj, proj, conv_w, conv_w, conv_b, conv_b, gates_rows, gates_rows, head_gain)


def _attn_kernel(q_ref, k_ref, v_ref, lq1_ref, lk1_ref, lq2_ref, lk2_ref, gain_ref, o_ref,
                 m_scr, l_scr, acc_scr, *, tq):
    qi = pl.program_id(2)
    ki = pl.program_id(3)
    scale = D_QK_DIM ** -0.5

    @pl.when(ki == 0)
    def _():
        m_scr[...] = jnp.full_like(m_scr, -jnp.inf)
        l_scr[...] = jnp.zeros_like(l_scr)
        acc_scr[...] = jnp.zeros_like(acc_scr)

    def update(masked):
        v = v_ref[...]
        for c in range(2):
            q = q_ref[:, c * D_QK_DIM:(c + 1) * D_QK_DIM]
            k = k_ref[:, c * D_QK_DIM:(c + 1) * D_QK_DIM]
            s = lax.dot_general(q, k, NT_DIMS, preferred_element_type=F32) * scale
            if masked:
                tch = lax.broadcasted_iota(jnp.int32, s.shape, 0) // CHUNK
                sch = lax.broadcasted_iota(jnp.int32, s.shape, 1) // CHUNK
                s = jnp.where(sch <= tch, s, -jnp.inf)
            m_old = m_scr[c]
            m_new = jnp.maximum(m_old, jnp.max(s, axis=1, keepdims=True))
            alpha = jnp.exp(m_old - m_new)
            p = jnp.exp(s - m_new)
            l_scr[c] = alpha * l_scr[c] + jnp.sum(p, axis=1, keepdims=True)
            acc_scr[c] = alpha * acc_scr[c] + jnp.dot(p.astype(BF16), v, preferred_element_type=F32)
            m_scr[c] = m_new

    @pl.when(ki < qi)
    def _():
        update(False)

    @pl.when(ki == qi)
    def _():
        update(True)

    @pl.when(ki == pl.num_programs(3) - 1)
    def _():
        lam = (jnp.exp(jnp.sum(lq1_ref[...] * lk1_ref[...], axis=1, keepdims=True))
               - jnp.exp(jnp.sum(lq2_ref[...] * lk2_ref[...], axis=1, keepdims=True)) + LAM_INIT)
        o = acc_scr[0] / l_scr[0] - lam * (acc_scr[1] / l_scr[1])
        o_ref[...] = (_rms(o, gain_ref[...]) * (1.0 - LAM_INIT)).astype(o_ref.dtype)


def _attn(proj, lam_q1, lam_k1, lam_q2, lam_k2, head_gain, tq):
    B, S, _ = proj.shape
    H = D_HEADS
    nq = S // tq
    blk0 = 4 * M_HEADS
    lam_spec = pl.BlockSpec((1, D_QK_DIM), lambda b, h, qi, ki: (0, 0))
    return pl.pallas_call(
        functools.partial(_attn_kernel, tq=tq),
        grid=(B, H, nq, nq),
        in_specs=[pl.BlockSpec((None, tq, 256), lambda b, h, qi, ki: (b, qi, blk0 + h)),
                  pl.BlockSpec((None, tq, 256), lambda b, h, qi, ki: (b, jnp.minimum(ki, qi), blk0 + H + h)),
                  pl.BlockSpec((None, tq, 256), lambda b, h, qi, ki: (b, jnp.minimum(ki, qi), blk0 + 2 * H + h)),
                  lam_spec, lam_spec, lam_spec, lam_spec,
                  pl.BlockSpec((1, D_V_DIM), lambda b, h, qi, ki: (0, h))],
        out_specs=pl.BlockSpec((None, tq, D_V_DIM), lambda b, h, qi, ki: (b, qi, h)),
        out_shape=jax.ShapeDtypeStruct((B, S, H * D_V_DIM), BF16),
        scratch_shapes=[pltpu.VMEM((2, tq, 1), F32), pltpu.VMEM((2, tq, 1), F32),
                        pltpu.VMEM((2, tq, D_V_DIM), F32)],
        compiler_params=_cparams(("parallel", "parallel", "parallel", "arbitrary")),
        name="diff_attn",
    )(proj, proj, proj, lam_q1, lam_k1, lam_q2, lam_k2, head_gain)


def _post_kernel(x_ref, ym_ref, yd_ref, gm_ref, gd_ref, gate1_ref, sc2_ref, sh2_ref, gpost_ref, gpre_ref,
                 wbm_ref, wbd_ref, wo_ref, wq_ref, x1_ref, h2_ref, qry_ref):
    bm = jnp.dot(ym_ref[...], wbm_ref[...], preferred_element_type=F32)
    bd = jnp.dot(yd_ref[...], wbd_ref[...], preferred_element_type=F32)
    merged = jax.nn.sigmoid(gm_ref[...].astype(F32)) * bm + jax.nn.sigmoid(gd_ref[...].astype(F32)) * bd
    y = jnp.dot(merged.astype(BF16), wo_ref[...], preferred_element_type=F32)
    x1 = x_ref[...] + gate1_ref[...] * _rms(y, gpost_ref[...])
    x1_ref[...] = x1
    h2 = (_rms(x1, gpre_ref[...]) * (1.0 + sc2_ref[...]) + sh2_ref[...]).astype(BF16)
    h2_ref[...] = h2
    qry_ref[...] = jnp.dot(h2, wq_ref[...], preferred_element_type=F32)


def _post(x, y_m, y_d, proj, gate1, scale2, shift2, g_post, g_pre, w_br_m, w_br_d, w_out, w_query, tm):
    B, S, D = x.shape
    nq = w_query.shape[1]
    row = pl.BlockSpec((None, tm, D), lambda b, i: (b, i, 0))
    per_b = pl.BlockSpec((None, 1, D), lambda b, i: (b, 0, 0))
    vec = pl.BlockSpec((1, D), lambda b, i: (0, 0))
    wsq = pl.BlockSpec((D, D), lambda b, i: (0, 0))
    return pl.pallas_call(
        _post_kernel,
        grid=(B, S // tm),
        in_specs=[row, row, row,
                  pl.BlockSpec((None, tm, D), lambda b, i: (b, i, 7)),
                  pl.BlockSpec((None, tm, D), lambda b, i: (b, i, 8)),
                  per_b, per_b, per_b, vec, vec, wsq, wsq, wsq,
                  pl.BlockSpec((D, nq), lambda b, i: (0, 0))],
        out_specs=[row, row, pl.BlockSpec((None, tm, nq), lambda b, i: (b, i, 0))],
        out_shape=[jax.ShapeDtypeStruct((B, S, D), F32),
                   jax.ShapeDtypeStruct((B, S, D), BF16),
                   jax.ShapeDtypeStruct((B, S, nq), F32)],
        compiler_params=_cparams(("parallel", "parallel")),
        name="post_mix",
    )(x, y_m, y_d, proj, proj, gate1, scale2, shift2, g_post, g_pre, w_br_m, w_br_d, w_out, w_query)


def _top16_rows(x):
    n_rows = x.shape[0]
    iota = lax.broadcasted_iota(jnp.int32, x.shape, 0).astype(F32)
    out = []
    for _ in range(P_TOPK):
        m = jnp.max(x, axis=0, keepdims=True)
        first = jnp.min(jnp.where(x == m, iota, float(n_rows)), axis=0, keepdims=True)
        x = jnp.where(iota == first, -jnp.inf, x)
        out.append(m)
    return jnp.concatenate(out, axis=0)


def _route_kernel(q_ref, keys_ref, at_ref, bt_ref, st_ref):
    tops = []
    for c, o_ref in ((0, at_ref), (1, bt_ref)):
        q = q_ref[:, c * P_HALF:(c + 1) * P_HALF].astype(keys_ref.dtype)
        sc = lax.dot_general(keys_ref[c], q, NT_DIMS, preferred_element_type=F32)
        o_ref[...] = sc
        tops.append(_top16_rows(sc))
    a_top, b_top = tops
    pieces = [a_top[0:1, :] + b_top, a_top[1:2, :] + b_top[0:8, :]]
    rid8 = lax.broadcasted_iota(jnp.int32, b_top[0:8, :].shape, 0)
    for pos in range(2, P_TOPK):
        n_valid = P_TOPK // (pos + 1)
        pieces.append(jnp.where(rid8 < n_valid, a_top[pos:pos + 1, :] + b_top[0:8, :], -jnp.inf))
    best = _top16_rows(jnp.concatenate(pieces, axis=0))
    z = jnp.sum(jnp.exp(best - best[0:1, :]), axis=0, keepdims=True)
    st_ref[0:1, :] = best[P_TOPK - 1:P_TOPK, :]
    st_ref[1:2, :] = a_top[0:1, :]
    st_ref[2:3, :] = b_top[0:1, :]
    st_ref[3:4, :] = 1.0 / z


def _route(qry, keys, tb):
    T = qry.shape[0]
    return pl.pallas_call(
        _route_kernel,
        grid=(T // tb, P_HEADS),
        in_specs=[pl.BlockSpec((tb, 2 * P_HALF), lambda i, h: (i, h)),
                  pl.BlockSpec((None, 2, N_KEYS, P_HALF), lambda i, h: (h, 0, 0, 0))],
        out_specs=[pl.BlockSpec((None, N_KEYS, tb), lambda i, h: (h, 0, i)),
                   pl.BlockSpec((None, N_KEYS, tb), lambda i, h: (h, 0, i)),
                   pl.BlockSpec((None, 4, tb), lambda i, h: (h, 0, i))],
        out_shape=[jax.ShapeDtypeStruct((P_HEADS, N_KEYS, T), F32),
                   jax.ShapeDtypeStruct((P_HEADS, N_KEYS, T), F32),
                   jax.ShapeDtypeStruct((P_HEADS, 4, T), F32)],
        compiler_params=_cparams(("parallel", "parallel")),
        name="peer_route",
    )(qry, keys)


def _peer_kernel(h_ref, u_ref, v_ref, at_ref, bt_ref, st_ref, x1_ref, gate2_ref, gpost_ref, o_ref,
                 act_scr, w_scr, acc_scr, ea_scr, eb_scr, *, rows_per_block):
    j = pl.program_id(1)

    @pl.when(j == 0)
    def _():
        acc_scr[...] = jnp.zeros_like(acc_scr)
        for h in range(P_HEADS):
            ea_scr[h] = jnp.exp(at_ref[h] - st_ref[h, 1:2, :])
            eb_scr[h] = jnp.exp(bt_ref[h] - st_ref[h, 2:3, :]) * st_ref[h, 3:4, :]

    act_scr[...] = lax.dot_general(u_ref[...], h_ref[...], NT_DIMS, preferred_element_type=F32)

    def body(r, carry):
        key_row = j * rows_per_block + r
        off = pl.multiple_of(r * N_KEYS, N_KEYS)
        gate = jnp.zeros((N_KEYS, act_scr.shape[1]), F32)
        for h in range(P_HEADS):
            s = at_ref[h, pl.ds(key_row, 1), :] + bt_ref[h]
            w = ea_scr[h, pl.ds(key_row, 1), :] * eb_scr[h]
            gate = gate + jnp.where(s >= st_ref[h, 0:1, :], w, 0.0)
        a = act_scr[pl.ds(off, N_KEYS), :]
        act = 0.5 * a * (1.0 + lax.erf(a * (2.0 ** -0.5)))
        w_scr[pl.ds(off, N_KEYS), :] = (act * gate).astype(BF16)
        return carry

    lax.fori_loop(0, rows_per_block, body, 0)
    acc_scr[...] += lax.dot_general(w_scr[...], v_ref[...], TN_DIMS, preferred_element_type=F32)

    @pl.when(j == pl.num_programs(1) - 1)
    def _():
        o_ref[...] = x1_ref[...] + gate2_ref[...] * _rms(acc_scr[...], gpost_ref[...])


def _peer(h2, u, v, a_t, b_t, stats, x1, gate2, g_post, seq_len, tb, eb):
    T, D = h2.shape
    E = u.shape[0]
    blocks_per_seq = seq_len // tb
    return pl.pallas_call(
        functools.partial(_peer_kernel, rows_per_block=eb // N_KEYS),
        grid=(T // tb, E // eb),
        in_specs=[pl.BlockSpec((tb, D), lambda i, j: (i, 0)),
                  pl.BlockSpec((eb, D), lambda i, j: (j, 0)),
                  pl.BlockSpec((eb, D), lambda i, j: (j, 0)),
                  pl.BlockSpec((P_HEADS, N_KEYS, tb), lambda i, j: (0, 0, i)),
                  pl.BlockSpec((P_HEADS, N_KEYS, tb), lambda i, j: (0, 0, i)),
                  pl.BlockSpec((P_HEADS, 4, tb), lambda i, j: (0, 0, i)),
                  pl.BlockSpec((tb, D), lambda i, j: (i, 0)),
                  pl.BlockSpec((None, 1, D), lambda i, j: (i // blocks_per_seq, 0, 0)),
                  pl.BlockSpec((1, D), lambda i, j: (0, 0))],
        out_specs=pl.BlockSpec((tb, D), lambda i, j: (i, 0)),
        out_shape=jax.ShapeDtypeStruct((T, D), F32),
        scratch_shapes=[pltpu.VMEM((eb, tb), F32), pltpu.VMEM((eb, tb), BF16), pltpu.VMEM((tb, D), F32),
                        pltpu.VMEM((P_HEADS, N_KEYS, tb), F32), pltpu.VMEM((P_HEADS, N_KEYS, tb), F32)],
        compiler_params=_cparams(("parallel", "arbitrary")),
        name="peer_dense",
    )(h2, u, v, a_t, b_t, stats, x1, gate2, g_post)


def _pick(n, prefs):
    for p in prefs:
        if n % p == 0:
            return p
    raise ValueError(f"no supported tile for extent {n}")


def kernel(x, c, w_ada, b_ada, g_pre_mix, g_post_mix, g_pre_ffn, g_post_ffn, w_in, b_if, conv_w, conv_b,
           m_head_gain, lam_q1, lam_k1, lam_q2, lam_k2, d_head_gain, w_br_m, w_br_d, w_out,
           w_query, sub_keys, expert_u, expert_v):
    B, S, D = x.shape
    depth = w_ada.shape[0]
    assert depth == 1 and B <= 8
    m_width = M_HEADS * M_HEAD_DIM
    gate_off = 4 * m_width
    n_gate = 2 * M_HEADS

    for l in range(depth):
        c8 = jnp.pad(c, ((0, 8 - B), (0, 0)))
        ada = _ada(c8, w_ada[l], b_ada[l][None, :])[:B]
        shift1, scale1, gate1, shift2, scale2, gate2 = [a[:, None, :] for a in jnp.split(ada, ADA_PARTS, axis=-1)]

        w_in_l = w_in[l]
        w_main = jnp.concatenate([w_in_l[:, :gate_off], w_in_l[:, gate_off + n_gate:]], axis=1).astype(BF16)
        w_if = jnp.pad(w_in_l[:, gate_off:gate_off + n_gate], ((0, 0), (0, 128 - n_gate))).astype(BF16)
        b_if_p = jnp.pad(b_if[l], (0, 128 - n_gate))[None, :]

        proj, gates = _inproj(x, g_pre_mix[l][None, :], scale1, shift1, w_main, w_if, b_if_p,
                              tm=_pick(S, (1024, 512, 256)), tn=_pick(w_main.shape[1], (2304, 1152, 1024)))
        gates_rows = jnp.transpose(gates[:, :, :n_gate], (0, 2, 1))[:, :, None, :]

        y_m = _mlstm(proj, gates_rows, conv_w[l], conv_b[l][None, :], m_head_gain[l][None, :],
                     L=_pick(S, (256,)))
        y_d = _attn(proj, lam_q1[l][None, :], lam_k1[l][None, :], lam_q2[l][None, :], lam_k2[l][None, :],
                    d_head_gain[l][None, :], tq=_pick(S, (512, 256)))

        x1, h2, qry = _post(x, y_m, y_d, proj, gate1, scale2, shift2, g_post_mix[l][None, :],
                            g_pre_ffn[l][None, :], w_br_m[l].astype(BF16), w_br_d[l].astype(BF16),
                            w_out[l].astype(BF16), w_query[l].astype(BF16), tm=_pick(S, (512, 256)))

        T = B * S
        a_t, b_t, stats = _route(qry.reshape(T, -1), sub_keys[l].astype(BF16), tb=_pick(T, (1024, 512)))
        x = _peer(h2.reshape(T, D), expert_u[l].astype(BF16), expert_v[l].astype(BF16), a_t, b_t, stats,
                  x1.reshape(T, D), gate2, g_post_ffn[l][None, :], seq_len=S,
                  tb=_pick(S, (512, 256)), eb=2048).reshape(B, S, D)
    return x
```

```python
import functools
import math

import jax
import jax.numpy as jnp
from jax import lax
from jax.experimental import pallas as pl
from jax.experimental.pallas import tpu as pltpu

F32 = jnp.float32
BF16 = jnp.bfloat16
HIGHEST = lax.Precision.HIGHEST

EPS = 1e-6
ADA_PARTS = 6
CHUNK = 64
M_HEADS = 4
M_HEAD_DIM = 256
CONV_WIDTH = 4
D_HEADS = 4
D_QK_DIM = 128
D_V_DIM = 256
P_HEADS = 8
N_KEYS = 128
P_TOPK = 16
P_HALF = 128
LAM_INIT = 0.8 - 0.6 * math.exp(-0.3 * 0)

V7X_VMEM_LIMIT_BYTES = 56 * 1024 * 1024

NT_DIMS = (((1,), (1,)), ((), ()))
TN_DIMS = (((0,), (0,)), ((), ()))


def _cparams(semantics):
    return pltpu.CompilerParams(dimension_semantics=semantics, vmem_limit_bytes=V7X_VMEM_LIMIT_BYTES)


def _rms(x, gain):
    return x * lax.rsqrt(jnp.mean(x * x, axis=-1, keepdims=True) + EPS) * gain


def _ada_kernel(c_ref, w_ref, b_ref, o_ref):
    c = c_ref[...]
    sc = c * jax.nn.sigmoid(c)
    o_ref[...] = jnp.dot(sc.astype(BF16), w_ref[...].astype(BF16), preferred_element_type=F32) + b_ref[...]


def _ada(c8, w_ada, b_ada):
    d = c8.shape[1]
    n = w_ada.shape[1]
    tn = 1024
    return pl.pallas_call(
        _ada_kernel,
        grid=(n // tn,),
        in_specs=[pl.BlockSpec((8, d), lambda j: (0, 0)),
                  pl.BlockSpec((d, tn), lambda j: (0, j)),
                  pl.BlockSpec((1, tn), lambda j: (0, j))],
        out_specs=pl.BlockSpec((8, tn), lambda j: (0, j)),
        out_shape=jax.ShapeDtypeStruct((8, n), F32),
        compiler_params=_cparams(("arbitrary",)),
        name="ada",
    )(c8, w_ada, b_ada)


def _inproj_kernel(x_ref, g_ref, sc_ref, sh_ref, w_ref, wif_ref, bif_ref, o_ref, oif_ref, h_scr):
    @pl.when(pl.program_id(2) == 0)
    def _():
        h = _rms(x_ref[...], g_ref[...]) * (1.0 + sc_ref[...]) + sh_ref[...]
        hb = h.astype(BF16)
        h_scr[...] = hb
        oif_ref[...] = jnp.dot(hb, wif_ref[...], preferred_element_type=F32) + bif_ref[...]

    o_ref[...] = jnp.dot(h_scr[...], w_ref[...], preferred_element_type=F32).astype(o_ref.dtype)


def _inproj(x, gain, scale, shift, w_main, w_if, b_if, tm, tn):
    B, S, D = x.shape
    n = w_main.shape[1]
    return pl.pallas_call(
        _inproj_kernel,
        grid=(B, S // tm, n // tn),
        in_specs=[pl.BlockSpec((None, tm, D), lambda b, i, j: (b, i, 0)),
                  pl.BlockSpec((1, D), lambda b, i, j: (0, 0)),
                  pl.BlockSpec((None, 1, D), lambda b, i, j: (b, 0, 0)),
                  pl.BlockSpec((None, 1, D), lambda b, i, j: (b, 0, 0)),
                  pl.BlockSpec((D, tn), lambda b, i, j: (0, j)),
                  pl.BlockSpec((D, 128), lambda b, i, j: (0, 0)),
                  pl.BlockSpec((1, 128), lambda b, i, j: (0, 0))],
        out_specs=[pl.BlockSpec((None, tm, tn), lambda b, i, j: (b, i, j)),
                   pl.BlockSpec((None, tm, 128), lambda b, i, j: (b, i, 0))],
        out_shape=[jax.ShapeDtypeStruct((B, S, n), BF16),
                   jax.ShapeDtypeStruct((B, S, 128), F32)],
        scratch_shapes=[pltpu.VMEM((tm, D), BF16)],
        compiler_params=_cparams(("parallel", "parallel", "arbitrary")),
        name="inproj",
    )(x, gain, scale, shift, w_main, w_if, b_if)


def _mlstm_kernel(q_ref, k_ref, v_ref, mo_ref, cwq_ref, cwk_ref, cbq_ref, cbk_ref, gi_ref, gf_ref, gain_ref,
                  y_ref, ct_scr, n_scr, m_scr, qtail, ktail, xext, *, L):
    @pl.when(pl.program_id(2) == 0)
    def _():
        ct_scr[...] = jnp.zeros_like(ct_scr)
        n_scr[...] = jnp.zeros_like(n_scr)
        m_scr[...] = jnp.zeros_like(m_scr)
        qtail[...] = jnp.zeros_like(qtail)
        ktail[...] = jnp.zeros_like(ktail)

    def conv_silu(x_ref, tail, cw_ref, cb_ref):
        x = x_ref[...].astype(F32)
        xext[0:8, :] = tail[...]
        xext[8:, :] = x
        acc = jnp.broadcast_to(cb_ref[...], x.shape)
        for j in range(CONV_WIDTH):
            acc = acc + cw_ref[j:j + 1, :] * xext[5 + j:5 + j + L, :]
        tail[...] = x[L - 8:, :]
        return acc * jax.nn.sigmoid(acc)

    q = conv_silu(q_ref, qtail, cwq_ref, cbq_ref)
    k = conv_silu(k_ref, ktail, cwk_ref, cbk_ref) * (M_HEAD_DIM ** -0.5)
    v = v_ref[...]
    qb = q.astype(BF16)
    kb = k.astype(BF16)

    li_row = gi_ref[...]
    fp = gf_ref[...]
    lf_row = jnp.minimum(fp, 0.0) - jnp.log1p(jnp.exp(-jnp.abs(fp)))

    ti = lax.broadcasted_iota(jnp.int32, (L, L), 0)
    si = lax.broadcasted_iota(jnp.int32, (L, L), 1)
    causal = si <= ti
    tril = causal.astype(F32)
    eye = (si == ti).astype(F32)
    rid = lax.broadcasted_iota(jnp.int32, (8, L), 0)
    rows = jnp.where(rid == 0, lf_row, jnp.where(rid == 1, li_row, 0.0))
    b_row = lax.dot_general(rows, tril, NT_DIMS, precision=HIGHEST, preferred_element_type=F32)[0:1, :]
    cum_cols = lax.dot_general(tril, rows, NT_DIMS, precision=HIGHEST, preferred_element_type=F32)
    raw_cols = lax.dot_general(eye, rows, NT_DIMS, precision=HIGHEST, preferred_element_type=F32)
    b_col = cum_cols[:, 0:1]
    li_col = raw_cols[:, 1:2]

    m_prev = m_scr[0:1, 0:1]
    a_col = b_col + m_prev
    dmat = jnp.where(causal, b_col - b_row + li_row, -jnp.inf)
    m_t = jnp.maximum(a_col, jnp.max(dmat, axis=1, keepdims=True))
    p = jnp.exp(dmat - m_t)
    s = lax.dot_general(qb, kb, NT_DIMS, preferred_element_type=F32) * p
    w_inter = jnp.exp(a_col - m_t)
    ct = ct_scr[...]
    num = (jnp.dot(s.astype(BF16), v, preferred_element_type=F32)
           + w_inter * jnp.dot(qb, ct.astype(BF16), preferred_element_type=F32))
    n_row = n_scr[...]
    den = jnp.sum(s, axis=1, keepdims=True) + w_inter * jnp.sum(q * n_row, axis=1, keepdims=True)
    h = num / jnp.maximum(jnp.abs(den), jnp.exp(-m_t))

    m_new = m_t[L - 1:L, :]
    b_last = b_col[L - 1:L, :]
    g_prev = jnp.exp(b_last + m_prev - m_new)
    gs_col = jnp.exp(b_last - b_col + li_col - m_new)
    gv = (gs_col * v.astype(F32)).astype(BF16)
    ct_scr[...] = g_prev * ct + lax.dot_general(kb, gv, TN_DIMS, preferred_element_type=F32)
    n_scr[...] = g_prev * n_row + jnp.sum(gs_col * k, axis=0, keepdims=True)
    m_scr[...] = jnp.broadcast_to(m_new, m_scr.shape)

    y = _rms(h, gain_ref[...]) * jax.nn.sigmoid(mo_ref[...].astype(F32))
    y_ref[...] = y.astype(y_ref.dtype)


def _mlstm(proj, gates_rows, conv_w, conv_b, head_gain, L):
    B, S, _ = proj.shape
    H, Dh = M_HEADS, M_HEAD_DIM
    col = lambda off: (lambda b, h, c: (b, c, off + h))
    return pl.pallas_call(
        functools.partial(_mlstm_kernel, L=L),
        grid=(B, H, S // L),
        in_specs=[pl.BlockSpec((None, L, Dh), col(0)),
                  pl.BlockSpec((None, L, Dh), col(H)),
                  pl.BlockSpec((None, L, Dh), col(2 * H)),
                  pl.BlockSpec((None, L, Dh), col(3 * H)),
                  pl.BlockSpec((CONV_WIDTH, Dh), lambda b, h, c: (0, h)),
                  pl.BlockSpec((CONV_WIDTH, Dh), lambda b, h, c: (0, H + h)),
                  pl.BlockSpec((1, Dh), lambda b, h, c: (0, h)),
                  pl.BlockSpec((1, Dh), lambda b, h, c: (0, H + h)),
                  pl.BlockSpec((None, None, 1, L), lambda b, h, c: (b, h, 0, c)),
                  pl.BlockSpec((None, None, 1, L), lambda b, h, c: (b, H + h, 0, c)),
                  pl.BlockSpec((1, Dh), lambda b, h, c: (0, h))],
        out_specs=pl.BlockSpec((None, L, Dh), lambda b, h, c: (b, c, h)),
        out_shape=jax.ShapeDtypeStruct((B, S, H * Dh), BF16),
        scratch_shapes=[pltpu.VMEM((Dh, Dh), F32), pltpu.VMEM((1, Dh), F32), pltpu.VMEM((8, 128), F32),
                        pltpu.VMEM((8, Dh), F32), pltpu.VMEM((8, Dh), F32), pltpu.VMEM((L + 8, Dh), F32)],
        compiler_params=_cparams(("parallel", "parallel", "arbitrary")),
        name="mlstm",
    )(proj, proj, proj, proj, conv_w, conv_w, conv_b, conv_b, gates_rows, gates_rows, head_gain)


def _attn_kernel(q_ref, k_ref, v_ref, lq1_ref, lk1_ref, lq2_ref, lk2_ref, gain_ref, o_ref,
                 m_scr, l_scr, acc_scr, *, tq, tk):
    qi = pl.program_id(2)
    c2 = (D_QK_DIM ** -0.5) * math.log2(math.e)

    m_scr[...] = jnp.full_like(m_scr, -jnp.inf)
    l_scr[...] = jnp.zeros_like(l_scr)
    acc_scr[...] = jnp.zeros_like(acc_scr)

    def update(k0, key_offset):
        v = v_ref[pl.ds(k0, tk), :]
        for c in range(2):
            q = q_ref[:, c * D_QK_DIM:(c + 1) * D_QK_DIM]
            k = k_ref[pl.ds(k0, tk), c * D_QK_DIM:(c + 1) * D_QK_DIM]
            s = lax.dot_general(q, k, NT_DIMS, preferred_element_type=F32)
            if key_offset is not None:
                tch = lax.broadcasted_iota(jnp.int32, s.shape, 0) // CHUNK
                sch = (lax.broadcasted_iota(jnp.int32, s.shape, 1) + key_offset) // CHUNK
                s = jnp.where(sch <= tch, s, -jnp.inf)
            m_old = m_scr[c]
            m_new = jnp.maximum(m_old, jnp.max(s, axis=1, keepdims=True))
            alpha = jnp.exp2((m_old - m_new) * c2)
            p = jnp.exp2((s - pltpu.repeat(m_new, tk // 128, axis=1)) * c2)
            p_lanes = p[:, 0:128]
            for g in range(1, tk // 128):
                p_lanes = p_lanes + p[:, g * 128:(g + 1) * 128]
            l_scr[c] = alpha * l_scr[c] + p_lanes
            acc_scr[c] = (pltpu.repeat(alpha, D_V_DIM // 128, axis=1) * acc_scr[c]
                          + jnp.dot(p.astype(BF16), v, preferred_element_type=F32))
            m_scr[c] = m_new

    def full_chunk(j, carry):
        update(pl.multiple_of(j * tk, tk), None)
        return carry

    lax.fori_loop(0, qi * (tq // tk), full_chunk, 0)
    for d in range(tq // tk):
        update(pl.multiple_of(qi * tq + d * tk, tk), d * tk)

    lam = (jnp.exp(jnp.sum(lq1_ref[...] * lk1_ref[...], axis=1, keepdims=True))
           - jnp.exp(jnp.sum(lq2_ref[...] * lk2_ref[...], axis=1, keepdims=True)) + LAM_INIT)
    l0 = jnp.sum(l_scr[0], axis=1, keepdims=True)
    l1 = jnp.sum(l_scr[1], axis=1, keepdims=True)
    o = acc_scr[0] / l0 - lam * (acc_scr[1] / l1)
    o_ref[...] = (_rms(o, gain_ref[...]) * (1.0 - LAM_INIT)).astype(o_ref.dtype)


def _attn(proj, lam_q1, lam_k1, lam_q2, lam_k2, head_gain, tq, tk):
    B, S, _ = proj.shape
    H = D_HEADS
    blk0 = 4 * M_HEADS
    lam_spec = pl.BlockSpec((1, D_QK_DIM), lambda b, h, qi: (0, 0))
    return pl.pallas_call(
        functools.partial(_attn_kernel, tq=tq, tk=tk),
        grid=(B, H, S // tq),
        in_specs=[pl.BlockSpec((None, tq, 256), lambda b, h, qi: (b, qi, blk0 + h)),
                  pl.BlockSpec((None, S, 256), lambda b, h, qi: (b, 0, blk0 + H + h)),
                  pl.BlockSpec((None, S, 256), lambda b, h, qi: (b, 0, blk0 + 2 * H + h)),
                  lam_spec, lam_spec, lam_spec, lam_spec,
                  pl.BlockSpec((1, D_V_DIM), lambda b, h, qi: (0, h))],
        out_specs=pl.BlockSpec((None, tq, D_V_DIM), lambda b, h, qi: (b, qi, h)),
        out_shape=jax.ShapeDtypeStruct((B, S, H * D_V_DIM), BF16),
        scratch_shapes=[pltpu.VMEM((2, tq, 128), F32), pltpu.VMEM((2, tq, 128), F32),
                        pltpu.VMEM((2, tq, D_V_DIM), F32)],
        compiler_params=_cparams(("parallel", "parallel", "arbitrary")),
        name="diff_attn",
    )(proj, proj, proj, lam_q1, lam_k1, lam_q2, lam_k2, head_gain)


def _post_kernel(x_ref, ym_ref, yd_ref, gm_ref, gd_ref, gate1_ref, sc2_ref, sh2_ref, gpost_ref, gpre_ref,
                 wbm_ref, wbd_ref, wo_ref, wq_ref, x1_ref, h2_ref, qry_ref):
    bm = jnp.dot(ym_ref[...], wbm_ref[...], preferred_element_type=F32)
    bd = jnp.dot(yd_ref[...], wbd_ref[...], preferred_element_type=F32)
    merged = jax.nn.sigmoid(gm_ref[...].astype(F32)) * bm + jax.nn.sigmoid(gd_ref[...].astype(F32)) * bd
    y = jnp.dot(merged.astype(BF16), wo_ref[...], preferred_element_type=F32)
    x1 = x_ref[...] + gate1_ref[...] * _rms(y, gpost_ref[...])
    x1_ref[...] = x1
    h2 = (_rms(x1, gpre_ref[...]) * (1.0 + sc2_ref[...]) + sh2_ref[...]).astype(BF16)
    h2_ref[...] = h2
    qry_ref[...] = jnp.dot(h2, wq_ref[...], preferred_element_type=F32)


def _post(x, y_m, y_d, proj, gate1, scale2, shift2, g_post, g_pre, w_br_m, w_br_d, w_out, w_query, tm):
    B, S, D = x.shape
    nq = w_query.shape[1]
    row = pl.BlockSpec((None, tm, D), lambda b, i: (b, i, 0))
    per_b = pl.BlockSpec((None, 1, D), lambda b, i: (b, 0, 0))
    vec = pl.BlockSpec((1, D), lambda b, i: (0, 0))
    wsq = pl.BlockSpec((D, D), lambda b, i: (0, 0))
    return pl.pallas_call(
        _post_kernel,
        grid=(B, S // tm),
        in_specs=[row, row, row,
                  pl.BlockSpec((None, tm, D), lambda b, i: (b, i, 7)),
                  pl.BlockSpec((None, tm, D), lambda b, i: (b, i, 8)),
                  per_b, per_b, per_b, vec, vec, wsq, wsq, wsq,
                  pl.BlockSpec((D, nq), lambda b, i: (0, 0))],
        out_specs=[row, row, pl.BlockSpec((None, tm, nq), lambda b, i: (b, i, 0))],
        out_shape=[jax.ShapeDtypeStruct((B, S, D), F32),
                   jax.ShapeDtypeStruct((B, S, D), BF16),
                   jax.ShapeDtypeStruct((B, S, nq), F32)],
        compiler_params=_cparams(("parallel", "parallel")),
        name="post_mix",
    )(x, y_m, y_d, proj, proj, gate1, scale2, shift2, g_post, g_pre, w_br_m, w_br_d, w_out, w_query)


def _top16_rows(x):
    n_rows = x.shape[0]
    iota = lax.broadcasted_iota(jnp.int32, x.shape, 0).astype(F32)
    out = []
    for _ in range(P_TOPK):
        m = jnp.max(x, axis=0, keepdims=True)
        first = jnp.min(jnp.where(x == m, iota, float(n_rows)), axis=0, keepdims=True)
        x = jnp.where(iota == first, -jnp.inf, x)
        out.append(m)
    return jnp.concatenate(out, axis=0)


def _route_kernel(q_ref, keys_ref, at_ref, bt_ref, st_ref):
    tops = []
    for c, o_ref in ((0, at_ref), (1, bt_ref)):
        q = q_ref[:, c * P_HALF:(c + 1) * P_HALF].astype(keys_ref.dtype)
        sc = lax.dot_general(keys_ref[c], q, NT_DIMS, preferred_element_type=F32)
        o_ref[...] = sc
        tops.append(_top16_rows(sc))
    a_top, b_top = tops
    pieces = [a_top[0:1, :] + b_top, a_top[1:2, :] + b_top[0:8, :]]
    rid8 = lax.broadcasted_iota(jnp.int32, b_top[0:8, :].shape, 0)
    for pos in range(2, P_TOPK):
        n_valid = P_TOPK // (pos + 1)
        pieces.append(jnp.where(rid8 < n_valid, a_top[pos:pos + 1, :] + b_top[0:8, :], -jnp.inf))
    best = _top16_rows(jnp.concatenate(pieces, axis=0))
    z = jnp.sum(jnp.exp(best - best[0:1, :]), axis=0, keepdims=True)
    st_ref[0:1, :] = best[P_TOPK - 1:P_TOPK, :]
    st_ref[1:2, :] = a_top[0:1, :]
    st_ref[2:3, :] = b_top[0:1, :]
    st_ref[3:4, :] = 1.0 / z


def _route(qry, keys, tb):
    T = qry.shape[0]
    return pl.pallas_call(
        _route_kernel,
        grid=(T // tb, P_HEADS),
        in_specs=[pl.BlockSpec((tb, 2 * P_HALF), lambda i, h: (i, h)),
                  pl.BlockSpec((None, 2, N_KEYS, P_HALF), lambda i, h: (h, 0, 0, 0))],
        out_specs=[pl.BlockSpec((None, N_KEYS, tb), lambda i, h: (h, 0, i)),
                   pl.BlockSpec((None, N_KEYS, tb), lambda i, h: (h, 0, i)),
                   pl.BlockSpec((None, 4, tb), lambda i, h: (h, 0, i))],
        out_shape=[jax.ShapeDtypeStruct((P_HEADS, N_KEYS, T), F32),
                   jax.ShapeDtypeStruct((P_HEADS, N_KEYS, T), F32),
                   jax.ShapeDtypeStruct((P_HEADS, 4, T), F32)],
        compiler_params=_cparams(("parallel", "parallel")),
        name="peer_route",
    )(qry, keys)


def _peer_kernel(h_ref, u_ref, v_ref, at_ref, bt_ref, st_ref, x1_ref, gate2_ref, gpost_ref, o_ref,
                 act_scr, w_scr, acc_scr, ea_scr, eb_scr, *, rows_per_block):
    j = pl.program_id(1)

    @pl.when(j == 0)
    def _():
        acc_scr[...] = jnp.zeros_like(acc_scr)
        for h in range(P_HEADS):
            ea_scr[h] = jnp.exp(at_ref[h] - st_ref[h, 1:2, :])
            eb_scr[h] = jnp.exp(bt_ref[h] - st_ref[h, 2:3, :]) * st_ref[h, 3:4, :]

    act_scr[...] = lax.dot_general(u_ref[...], h_ref[...], NT_DIMS, preferred_element_type=F32)

    def body(r, carry):
        key_row = j * rows_per_block + r
        off = pl.multiple_of(r * N_KEYS, N_KEYS)
        gate = jnp.zeros((N_KEYS, act_scr.shape[1]), F32)
        for h in range(P_HEADS):
            s = at_ref[h, pl.ds(key_row, 1), :] + bt_ref[h]
            w = ea_scr[h, pl.ds(key_row, 1), :] * eb_scr[h]
            gate = gate + jnp.where(s >= st_ref[h, 0:1, :], w, 0.0)
        a = act_scr[pl.ds(off, N_KEYS), :]
        act = 0.5 * a * (1.0 + lax.erf(a * (2.0 ** -0.5)))
        w_scr[pl.ds(off, N_KEYS), :] = (act * gate).astype(BF16)
        return carry

    lax.fori_loop(0, rows_per_block, body, 0)
    acc_scr[...] += lax.dot_general(w_scr[...], v_ref[...], TN_DIMS, preferred_element_type=F32)

    @pl.when(j == pl.num_programs(1) - 1)
    def _():
        o_ref[...] = x1_ref[...] + gate2_ref[...] * _rms(acc_scr[...], gpost_ref[...])


def _peer(h2, u, v, a_t, b_t, stats, x1, gate2, g_post, seq_len, tb, eb):
    T, D = h2.shape
    E = u.shape[0]
    blocks_per_seq = seq_len // tb
    return pl.pallas_call(
        functools.partial(_peer_kernel, rows_per_block=eb // N_KEYS),
        grid=(T // tb, E // eb),
        in_specs=[pl.BlockSpec((tb, D), lambda i, j: (i, 0)),
                  pl.BlockSpec((eb, D), lambda i, j: (j, 0)),
                  pl.BlockSpec((eb, D), lambda i, j: (j, 0)),
                  pl.BlockSpec((P_HEADS, N_KEYS, tb), lambda i, j: (0, 0, i)),
                  pl.BlockSpec((P_HEADS, N_KEYS, tb), lambda i, j: (0, 0, i)),
                  pl.BlockSpec((P_HEADS, 4, tb), lambda i, j: (0, 0, i)),
                  pl.BlockSpec((tb, D), lambda i, j: (i, 0)),
                  pl.BlockSpec((None, 1, D), lambda i, j: (i // blocks_per_seq, 0, 0)),
                  pl.BlockSpec((1, D), lambda i, j: (0, 0))],
        out_specs=pl.BlockSpec((tb, D), lambda i, j: (i, 0)),
        out_shape=jax.ShapeDtypeStruct((T, D), F32),
        scratch_shapes=[pltpu.VMEM((eb, tb), F32), pltpu.VMEM((eb, tb), BF16), pltpu.VMEM((tb, D), F32),
                        pltpu.VMEM((P_HEADS, N_KEYS, tb), F32), pltpu.VMEM((P_HEADS, N_KEYS, tb), F32)],
        compiler_params=_cparams(("parallel", "arbitrary")),
        name="peer_dense",
    )(h2, u, v, a_t, b_t, stats, x1, gate2, g_post)


def _pick(n, prefs):
    for p in prefs:
        if n % p == 0:
            return p
    raise ValueError(f"no supported tile for extent {n}")


def kernel(x, c, w_ada, b_ada, g_pre_mix, g_post_mix, g_pre_ffn, g_post_ffn, w_in, b_if, conv_w, conv_b,
           m_head_gain, lam_q1, lam_k1, lam_q2, lam_k2, d_head_gain, w_br_m, w_br_d, w_out,
           w_query, sub_keys, expert_u, expert_v):
    B, S, D = x.shape
    depth = w_ada.shape[0]
    assert depth == 1 and B <= 8
    m_width = M_HEADS * M_HEAD_DIM
    gate_off = 4 * m_width
    n_gate = 2 * M_HEADS

    for l in range(depth):
        c8 = jnp.pad(c, ((0, 8 - B), (0, 0)))
        ada = _ada(c8, w_ada[l], b_ada[l][None, :])[:B]
        shift1, scale1, gate1, shift2, scale2, gate2 = [a[:, None, :] for a in jnp.split(ada, ADA_PARTS, axis=-1)]

        w_in_l = w_in[l]
        w_main = jnp.concatenate([w_in_l[:, :gate_off], w_in_l[:, gate_off + n_gate:]], axis=1).astype(BF16)
        w_if = jnp.pad(w_in_l[:, gate_off:gate_off + n_gate], ((0, 0), (0, 128 - n_gate))).astype(BF16)
        b_if_p = jnp.pad(b_if[l], (0, 128 - n_gate))[None, :]

        proj, gates = _inproj(x, g_pre_mix[l][None, :], scale1, shift1, w_main, w_if, b_if_p,
                              tm=_pick(S, (1024, 512, 256)), tn=_pick(w_main.shape[1], (2304, 1152, 1024)))
        gates_rows = jnp.transpose(gates[:, :, :n_gate], (0, 2, 1))[:, :, None, :]

        y_m = _mlstm(proj, gates_rows, conv_w[l], conv_b[l][None, :], m_head_gain[l][None, :],
                     L=_pick(S, (256,)))
        y_d = _attn(proj, lam_q1[l][None, :], lam_k1[l][None, :], lam_q2[l][None, :], lam_k2[l][None, :],
                    d_head_gain[l][None, :], tq=_pick(S, (512, 256)), tk=_pick(S, (512, 256)))

        x1, h2, qry = _post(x, y_m, y_d, proj, gate1, scale2, shift2, g_post_mix[l][None, :],
                            g_pre_ffn[l][None, :], w_br_m[l].astype(BF16), w_br_d[l].astype(BF16),
                            w_out[l].astype(BF16), w_query[l].astype(BF16), tm=_pick(S, (512, 256)))

        T = B * S
        a_t, b_t, stats = _route(qry.reshape(T, -1), sub_keys[l].astype(BF16), tb=_pick(T, (1024, 512)))
        x = _peer(h2.reshape(T, D), expert_u[l].astype(BF16), expert_v[l].astype(BF16), a_t, b_t, stats,
                  x1.reshape(T, D), gate2, g_post_ffn[l][None, :], seq_len=S,
                  tb=_pick(S, (512, 256)), eb=2048).reshape(B, S, D)
    return x
```

```python
import functools
import math

import jax
import jax.numpy as jnp
from jax import lax
from jax.experimental import pallas as pl
from jax.experimental.pallas import tpu as pltpu

F32 = jnp.float32
BF16 = jnp.bfloat16
HIGHEST = lax.Precision.HIGHEST

EPS = 1e-6
ADA_PARTS = 6
CHUNK = 64
M_HEADS = 4
M_HEAD_DIM = 256
CONV_WIDTH = 4
D_HEADS = 4
D_QK_DIM = 128
D_V_DIM = 256
P_HEADS = 8
N_KEYS = 128
P_TOPK = 16
P_HALF = 128
LAM_INIT = 0.8 - 0.6 * math.exp(-0.3 * 0)

V7X_VMEM_LIMIT_BYTES = 56 * 1024 * 1024

NT_DIMS = (((1,), (1,)), ((), ()))
TN_DIMS = (((0,), (0,)), ((), ()))


def _cparams(semantics):
    return pltpu.CompilerParams(dimension_semantics=semantics, vmem_limit_bytes=V7X_VMEM_LIMIT_BYTES)


def _rms(x, gain):
    return x * lax.rsqrt(jnp.mean(x * x, axis=-1, keepdims=True) + EPS) * gain


def _ada_kernel(c_ref, w_ref, b_ref, o_ref):
    c = c_ref[...]
    sc = c * jax.nn.sigmoid(c)
    o_ref[...] = jnp.dot(sc.astype(BF16), w_ref[...].astype(BF16), preferred_element_type=F32) + b_ref[...]


def _ada(c8, w_ada, b_ada):
    d = c8.shape[1]
    n = w_ada.shape[1]
    tn = 1024
    return pl.pallas_call(
        _ada_kernel,
        grid=(n // tn,),
        in_specs=[pl.BlockSpec((8, d), lambda j: (0, 0)),
                  pl.BlockSpec((d, tn), lambda j: (0, j)),
                  pl.BlockSpec((1, tn), lambda j: (0, j))],
        out_specs=pl.BlockSpec((8, tn), lambda j: (0, j)),
        out_shape=jax.ShapeDtypeStruct((8, n), F32),
        compiler_params=_cparams(("arbitrary",)),
        name="ada",
    )(c8, w_ada, b_ada)


def _inproj_kernel(x_ref, g_ref, sc_ref, sh_ref, w_ref, wif_ref, bif_ref, o_ref, oif_ref, h_scr):
    @pl.when(pl.program_id(2) == 0)
    def _():
        h = _rms(x_ref[...], g_ref[...]) * (1.0 + sc_ref[...]) + sh_ref[...]
        hb = h.astype(BF16)
        h_scr[...] = hb
        oif_ref[...] = jnp.dot(hb, wif_ref[...], preferred_element_type=F32) + bif_ref[...]

    o_ref[...] = jnp.dot(h_scr[...], w_ref[...], preferred_element_type=F32).astype(o_ref.dtype)


def _inproj(x, gain, scale, shift, w_main, w_if, b_if, tm, tn):
    B, S, D = x.shape
    n = w_main.shape[1]
    return pl.pallas_call(
        _inproj_kernel,
        grid=(B, S // tm, n // tn),
        in_specs=[pl.BlockSpec((None, tm, D), lambda b, i, j: (b, i, 0)),
                  pl.BlockSpec((1, D), lambda b, i, j: (0, 0)),
                  pl.BlockSpec((None, 1, D), lambda b, i, j: (b, 0, 0)),
                  pl.BlockSpec((None, 1, D), lambda b, i, j: (b, 0, 0)),
                  pl.BlockSpec((D, tn), lambda b, i, j: (0, j)),
                  pl.BlockSpec((D, 128), lambda b, i, j: (0, 0)),
                  pl.BlockSpec((1, 128), lambda b, i, j: (0, 0))],
        out_specs=[pl.BlockSpec((None, tm, tn), lambda b, i, j: (b, i, j)),
                   pl.BlockSpec((None, tm, 128), lambda b, i, j: (b, i, 0))],
        out_shape=[jax.ShapeDtypeStruct((B, S, n), BF16),
                   jax.ShapeDtypeStruct((B, S, 128), F32)],
        scratch_shapes=[pltpu.VMEM((tm, D), BF16)],
        compiler_params=_cparams(("parallel", "parallel", "arbitrary")),
        name="inproj",
    )(x, gain, scale, shift, w_main, w_if, b_if)


def _mlstm_kernel(q_ref, k_ref, v_ref, mo_ref, cwq_ref, cwk_ref, cbq_ref, cbk_ref, gi_ref, gf_ref, gain_ref,
                  y_ref, ct_scr, n_scr, m_scr, qtail, ktail, xext, *, L):
    @pl.when(pl.program_id(2) == 0)
    def _():
        ct_scr[...] = jnp.zeros_like(ct_scr)
        n_scr[...] = jnp.zeros_like(n_scr)
        m_scr[...] = jnp.zeros_like(m_scr)
        qtail[...] = jnp.zeros_like(qtail)
        ktail[...] = jnp.zeros_like(ktail)

    def conv_silu(x_ref, tail, cw_ref, cb_ref):
        x = x_ref[...].astype(F32)
        xext[0:8, :] = tail[...]
        xext[8:, :] = x
        acc = jnp.broadcast_to(cb_ref[...], x.shape)
        for j in range(CONV_WIDTH):
            acc = acc + cw_ref[j:j + 1, :] * xext[5 + j:5 + j + L, :]
        tail[...] = x[L - 8:, :]
        return acc * jax.nn.sigmoid(acc)

    q = conv_silu(q_ref, qtail, cwq_ref, cbq_ref)
    k = conv_silu(k_ref, ktail, cwk_ref, cbk_ref) * (M_HEAD_DIM ** -0.5)
    v = v_ref[...]
    qb = q.astype(BF16)
    kb = k.astype(BF16)

    li_row = gi_ref[...]
    fp = gf_ref[...]
    lf_row = jnp.minimum(fp, 0.0) - jnp.log1p(jnp.exp(-jnp.abs(fp)))

    ti = lax.broadcasted_iota(jnp.int32, (L, L), 0)
    si = lax.broadcasted_iota(jnp.int32, (L, L), 1)
    causal = si <= ti
    tril = causal.astype(F32)
    eye = (si == ti).astype(F32)
    rid = lax.broadcasted_iota(jnp.int32, (8, L), 0)
    rows = jnp.where(rid == 0, lf_row, jnp.where(rid == 1, li_row, 0.0))
    b_row = lax.dot_general(rows, tril, NT_DIMS, precision=HIGHEST, preferred_element_type=F32)[0:1, :]
    cum_cols = lax.dot_general(tril, rows, NT_DIMS, precision=HIGHEST, preferred_element_type=F32)
    raw_cols = lax.dot_general(eye, rows, NT_DIMS, precision=HIGHEST, preferred_element_type=F32)
    b_col = cum_cols[:, 0:1]
    li_col = raw_cols[:, 1:2]

    m_prev = m_scr[0:1, 0:1]
    a_col = b_col + m_prev
    dmat = jnp.where(causal, b_col - b_row + li_row, -jnp.inf)
    m_t = jnp.maximum(a_col, jnp.max(dmat, axis=1, keepdims=True))
    p = jnp.exp(dmat - m_t)
    s = lax.dot_general(qb, kb, NT_DIMS, preferred_element_type=F32) * p
    w_inter = jnp.exp(a_col - m_t)
    ct = ct_scr[...]
    num = (jnp.dot(s.astype(BF16), v, preferred_element_type=F32)
           + w_inter * jnp.dot(qb, ct.astype(BF16), preferred_element_type=F32))
    n_row = n_scr[...]
    den = jnp.sum(s, axis=1, keepdims=True) + w_inter * jnp.sum(q * n_row, axis=1, keepdims=True)
    h = num / jnp.maximum(jnp.abs(den), jnp.exp(-m_t))

    m_new = m_t[L - 1:L, :]
    b_last = b_col[L - 1:L, :]
    g_prev = jnp.exp(b_last + m_prev - m_new)
    gs_col = jnp.exp(b_last - b_col + li_col - m_new)
    gv = (gs_col * v.astype(F32)).astype(BF16)
    ct_scr[...] = g_prev * ct + lax.dot_general(kb, gv, TN_DIMS, preferred_element_type=F32)
    n_scr[...] = g_prev * n_row + jnp.sum(gs_col * k, axis=0, keepdims=True)
    m_scr[...] = jnp.broadcast_to(m_new, m_scr.shape)

    y = _rms(h, gain_ref[...]) * jax.nn.sigmoid(mo_ref[...].astype(F32))
    y_ref[...] = y.astype(y_ref.dtype)


def _mlstm(proj, gates_rows, conv_w, conv_b, head_gain, L):
    B, S, _ = proj.shape
    H, Dh = M_HEADS, M_HEAD_DIM
    col = lambda off: (lambda b, h, c: (b, c, off + h))
    return pl.pallas_call(
        functools.partial(_mlstm_kernel, L=L),
        grid=(B, H, S // L),
        in_specs=[pl.BlockSpec((None, L, Dh), col(0)),
                  pl.BlockSpec((None, L, Dh), col(H)),
                  pl.BlockSpec((None, L, Dh), col(2 * H)),
                  pl.BlockSpec((None, L, Dh), col(3 * H)),
                  pl.BlockSpec((CONV_WIDTH, Dh), lambda b, h, c: (0, h)),
                  pl.BlockSpec((CONV_WIDTH, Dh), lambda b, h, c: (0, H + h)),
                  pl.BlockSpec((1, Dh), lambda b, h, c: (0, h)),
                  pl.BlockSpec((1, Dh), lambda b, h, c: (0, H + h)),
                  pl.BlockSpec((None, None, 1, L), lambda b, h, c: (b, h, 0, c)),
                  pl.BlockSpec((None, None, 1, L), lambda b, h, c: (b, H + h, 0, c)),
                  pl.BlockSpec((1, Dh), lambda b, h, c: (0, h))],
        out_specs=pl.BlockSpec((None, L, Dh), lambda b, h, c: (b, c, h)),
        out_shape=jax.ShapeDtypeStruct((B, S, H * Dh), BF16),
        scratch_shapes=[pltpu.VMEM((Dh, Dh), F32), pltpu.VMEM((1, Dh), F32), pltpu.VMEM((8, 128), F32),
                        pltpu.VMEM((8, Dh), F32), pltpu.VMEM((8, Dh), F32), pltpu.VMEM((L + 8, Dh), F32)],
        compiler_params=_cparams(("parallel", "parallel", "arbitrary")),
        name="mlstm",
    )(proj, proj, proj, proj, conv_w, conv_w, conv_b, conv_b, gates_rows, gates_rows, head_gain)


def _attn_kernel(q_ref, k_ref, v_ref, lq1_ref, lk1_ref, lq2_ref, lk2_ref, gain_ref, o_ref,
                 m_scr, l_scr, acc_scr, *, tq, tk):
    qi = pl.program_id(2)
    c2 = (D_QK_DIM ** -0.5) * math.log2(math.e)

    m_scr[...] = jnp.full_like(m_scr, -jnp.inf)
    l_scr[...] = jnp.zeros_like(l_scr)
    acc_scr[...] = jnp.zeros_like(acc_scr)

    def update(k0, key_offset):
        v = v_ref[pl.ds(k0, tk), :]
        for c in range(2):
            q = q_ref[:, c * D_QK_DIM:(c + 1) * D_QK_DIM]
            k = k_ref[pl.ds(k0, tk), c * D_QK_DIM:(c + 1) * D_QK_DIM]
            s = lax.dot_general(q, k, NT_DIMS, preferred_element_type=F32)
            if key_offset is not None:
                tch = lax.broadcasted_iota(jnp.int32, s.shape, 0) // CHUNK
                sch = (lax.broadcasted_iota(jnp.int32, s.shape, 1) + key_offset) // CHUNK
                s = jnp.where(sch <= tch, s, -jnp.inf)
            m_old = m_scr[c]
            m_new = jnp.maximum(m_old, jnp.max(s, axis=1, keepdims=True))
            alpha = jnp.exp2((m_old - m_new) * c2)
            p = jnp.exp2((s - jnp.concatenate([m_new] * (tk // 128), axis=1)) * c2)
            p_lanes = p[:, 0:128]
            for g in range(1, tk // 128):
                p_lanes = p_lanes + p[:, g * 128:(g + 1) * 128]
            l_scr[c] = alpha * l_scr[c] + p_lanes
            acc_scr[c] = (jnp.concatenate([alpha] * (D_V_DIM // 128), axis=1) * acc_scr[c]
                          + jnp.dot(p.astype(BF16), v, preferred_element_type=F32))
            m_scr[c] = m_new

    def full_chunk(j, carry):
        update(pl.multiple_of(j * tk, tk), None)
        return carry

    lax.fori_loop(0, qi * (tq // tk), full_chunk, 0)
    for d in range(tq // tk):
        update(pl.multiple_of(qi * tq + d * tk, tk), d * tk)

    lam = (jnp.exp(jnp.sum(lq1_ref[...] * lk1_ref[...], axis=1, keepdims=True))
           - jnp.exp(jnp.sum(lq2_ref[...] * lk2_ref[...], axis=1, keepdims=True)) + LAM_INIT)
    l0 = jnp.sum(l_scr[0], axis=1, keepdims=True)
    l1 = jnp.sum(l_scr[1], axis=1, keepdims=True)
    o = acc_scr[0] / l0 - lam * (acc_scr[1] / l1)
    o_ref[...] = (_rms(o, gain_ref[...]) * (1.0 - LAM_INIT)).astype(o_ref.dtype)


def _attn(proj, lam_q1, lam_k1, lam_q2, lam_k2, head_gain, tq, tk):
    B, S, _ = proj.shape
    H = D_HEADS
    blk0 = 4 * M_HEADS
    lam_spec = pl.BlockSpec((1, D_QK_DIM), lambda b, h, qi: (0, 0))
    return pl.pallas_call(
        functools.partial(_attn_kernel, tq=tq, tk=tk),
        grid=(B, H, S // tq),
        in_specs=[pl.BlockSpec((None, tq, 256), lambda b, h, qi: (b, qi, blk0 + h)),
                  pl.BlockSpec((None, S, 256), lambda b, h, qi: (b, 0, blk0 + H + h)),
                  pl.BlockSpec((None, S, 256), lambda b, h, qi: (b, 0, blk0 + 2 * H + h)),
                  lam_spec, lam_spec, lam_spec, lam_spec,
                  pl.BlockSpec((1, D_V_DIM), lambda b, h, qi: (0, h))],
        out_specs=pl.BlockSpec((None, tq, D_V_DIM), lambda b, h, qi: (b, qi, h)),
        out_shape=jax.ShapeDtypeStruct((B, S, H * D_V_DIM), BF16),
        scratch_shapes=[pltpu.VMEM((2, tq, 128), F32), pltpu.VMEM((2, tq, 128), F32),
                        pltpu.VMEM((2, tq, D_V_DIM), F32)],
        compiler_params=_cparams(("parallel", "parallel", "arbitrary")),
        name="diff_attn",
    )(proj, proj, proj, lam_q1, lam_k1, lam_q2, lam_k2, head_gain)


def _post_kernel(x_ref, ym_ref, yd_ref, gm_ref, gd_ref, gate1_ref, sc2_ref, sh2_ref, gpost_ref, gpre_ref,
                 wbm_ref, wbd_ref, wo_ref, wq_ref, x1_ref, h2_ref, qry_ref):
    bm = jnp.dot(ym_ref[...], wbm_ref[...], preferred_element_type=F32)
    bd = jnp.dot(yd_ref[...], wbd_ref[...], preferred_element_type=F32)
    merged = jax.nn.sigmoid(gm_ref[...].astype(F32)) * bm + jax.nn.sigmoid(gd_ref[...].astype(F32)) * bd
    y = jnp.dot(merged.astype(BF16), wo_ref[...], preferred_element_type=F32)
    x1 = x_ref[...] + gate1_ref[...] * _rms(y, gpost_ref[...])
    x1_ref[...] = x1
    h2 = (_rms(x1, gpre_ref[...]) * (1.0 + sc2_ref[...]) + sh2_ref[...]).astype(BF16)
    h2_ref[...] = h2
    qry_ref[...] = jnp.dot(h2, wq_ref[...], preferred_element_type=F32)


def _post(x, y_m, y_d, proj, gate1, scale2, shift2, g_post, g_pre, w_br_m, w_br_d, w_out, w_query, tm):
    B, S, D = x.shape
    nq = w_query.shape[1]
    row = pl.BlockSpec((None, tm, D), lambda b, i: (b, i, 0))
    per_b = pl.BlockSpec((None, 1, D), lambda b, i: (b, 0, 0))
    vec = pl.BlockSpec((1, D), lambda b, i: (0, 0))
    wsq = pl.BlockSpec((D, D), lambda b, i: (0, 0))
    return pl.pallas_call(
        _post_kernel,
        grid=(B, S // tm),
        in_specs=[row, row, row,
                  pl.BlockSpec((None, tm, D), lambda b, i: (b, i, 7)),
                  pl.BlockSpec((None, tm, D), lambda b, i: (b, i, 8)),
                  per_b, per_b, per_b, vec, vec, wsq, wsq, wsq,
                  pl.BlockSpec((D, nq), lambda b, i: (0, 0))],
        out_specs=[row, row, pl.BlockSpec((None, tm, nq), lambda b, i: (b, i, 0))],
        out_shape=[jax.ShapeDtypeStruct((B, S, D), F32),
                   jax.ShapeDtypeStruct((B, S, D), BF16),
                   jax.ShapeDtypeStruct((B, S, nq), F32)],
        compiler_params=_cparams(("parallel", "parallel")),
        name="post_mix",
    )(x, y_m, y_d, proj, proj, gate1, scale2, shift2, g_post, g_pre, w_br_m, w_br_d, w_out, w_query)


SUBLANES = 8


def _sort16_pairs():
    n, out, p = P_TOPK, [], 1
    while p < n:
        k = p
        while k >= 1:
            for j in range(k % p, n - k, 2 * k):
                for i in range(min(k, n - j - k)):
                    if (i + j) // (2 * p) == (i + j + k) // (2 * p):
                        out.append((i + j, i + j + k))
            k //= 2
        p *= 2
    return out


def _exchange(vs, i, j):
    hi, lo = jnp.maximum(vs[i], vs[j]), jnp.minimum(vs[i], vs[j])
    vs[i], vs[j] = hi, lo


def _merge_over_sublanes(vs):
    for shift in (4, 2, 1):
        other = [pltpu.roll(v, shift, axis=0) for v in vs]
        vs = [jnp.maximum(vs[k], other[P_TOPK - 1 - k]) for k in range(P_TOPK)]
        d = P_TOPK // 2
        while d >= 1:
            for k in range(P_TOPK):
                if k & d == 0:
                    _exchange(vs, k, k + d)
            d //= 2
    return vs


def _top16_of_keys(slices):
    vs = list(slices)
    for i, j in _sort16_pairs():
        _exchange(vs, i, j)
    return _merge_over_sublanes(vs)


def _route_kernel(q_ref, keys_ref, bt_ref, beta_ref, ea_ref, eb_ref):
    n_groups = N_KEYS // SUBLANES
    raw = []
    for c in range(2):
        q = q_ref[:, c * P_HALF:(c + 1) * P_HALF].astype(keys_ref.dtype)
        sc = lax.dot_general(keys_ref[c], q, NT_DIMS, preferred_element_type=F32)
        raw.append([sc[g * SUBLANES:(g + 1) * SUBLANES, :] for g in range(n_groups)])
    a_rows, b_rows = raw
    a_top = _top16_of_keys(a_rows)
    b_top = _top16_of_keys(b_rows)

    sub = lax.broadcasted_iota(jnp.int32, a_top[0].shape, 0)
    b_lo, b_hi = b_top[0], b_top[SUBLANES]
    for qq in range(1, SUBLANES):
        b_lo = jnp.where(sub == qq, b_top[qq], b_lo)
        b_hi = jnp.where(sub == qq, b_top[SUBLANES + qq], b_hi)
    cand = []
    for p in range(P_TOPK):
        n_valid = P_TOPK // (p + 1)
        cp = a_top[p] + b_lo
        cand.append(cp if n_valid >= SUBLANES else jnp.where(sub < n_valid, cp, -jnp.inf))
    extra = a_top[0] + b_hi
    for k in range(P_TOPK):
        cand[k], extra = jnp.maximum(cand[k], extra), jnp.minimum(cand[k], extra)
    best = _merge_over_sublanes(cand)

    tau = best[P_TOPK - 1]
    z = jnp.ones_like(tau)
    for k in range(1, P_TOPK):
        z = z + jnp.exp(best[k] - best[0])
    inv_z = 1.0 / z
    for g in range(n_groups):
        rows = pl.ds(g * SUBLANES, SUBLANES)
        beta = jnp.full_like(tau, jnp.inf)
        for qq in range(P_TOPK):
            beta = jnp.where(a_rows[g] + b_top[qq] >= tau, b_top[qq], beta)
        beta_ref[rows, :] = beta
        bt_ref[rows, :] = b_rows[g]
        ea_ref[rows, :] = jnp.exp(a_rows[g] - a_top[0])
        eb_ref[rows, :] = jnp.exp(b_rows[g] - b_top[0]) * inv_z


def _route(qry, keys, tb):
    T = qry.shape[0]
    out_spec = pl.BlockSpec((None, N_KEYS, tb), lambda i, h: (h, 0, i))
    out_shape = jax.ShapeDtypeStruct((P_HEADS, N_KEYS, T), F32)
    return pl.pallas_call(
        _route_kernel,
        grid=(T // tb, P_HEADS),
        in_specs=[pl.BlockSpec((tb, 2 * P_HALF), lambda i, h: (i, h)),
                  pl.BlockSpec((None, 2, N_KEYS, P_HALF), lambda i, h: (h, 0, 0, 0))],
        out_specs=[out_spec] * 4,
        out_shape=[out_shape] * 4,
        compiler_params=_cparams(("parallel", "parallel")),
        name="peer_route",
    )(qry, keys)


def _peer_kernel(h_ref, u_ref, v_ref, bt_ref, beta_ref, ea_ref, eb_ref, x1_ref, gate2_ref, gpost_ref, o_ref,
                 act_scr, w_scr, acc_scr, *, rows_per_block):
    j = pl.program_id(1)

    @pl.when(j == 0)
    def _():
        acc_scr[...] = jnp.zeros_like(acc_scr)

    act_scr[...] = lax.dot_general(u_ref[...], h_ref[...], NT_DIMS, preferred_element_type=F32)

    def row_group(r, carry):
        row0 = pl.multiple_of(r * N_KEYS, N_KEYS)
        key_row = j * rows_per_block + r
        gate = None
        for h in range(P_HEADS):
            picked = bt_ref[h] >= beta_ref[h, pl.ds(key_row, 1), :]
            term = ea_ref[h, pl.ds(key_row, 1), :] * jnp.where(picked, eb_ref[h], 0.0)
            gate = term if gate is None else gate + term
        a = act_scr[pl.ds(row0, N_KEYS), :]
        gelu = a * (0.5 + 0.5 * lax.erf(a * (2.0 ** -0.5)))
        w_scr[pl.ds(row0, N_KEYS), :] = (gelu * gate).astype(BF16)
        return carry

    lax.fori_loop(0, rows_per_block, row_group, 0)
    acc_scr[...] += lax.dot_general(w_scr[...], v_ref[...], TN_DIMS, preferred_element_type=F32)

    @pl.when(j == pl.num_programs(1) - 1)
    def _():
        o_ref[...] = x1_ref[...] + gate2_ref[...] * _rms(acc_scr[...], gpost_ref[...])


def _peer(h2, u, v, b_t, beta_t, ea_t, eb_t, x1, gate2, g_post, seq_len, tb, eb):
    T, D = h2.shape
    E = u.shape[0]
    blocks_per_seq = seq_len // tb
    route_spec = pl.BlockSpec((P_HEADS, N_KEYS, tb), lambda i, j: (0, 0, i))
    return pl.pallas_call(
        functools.partial(_peer_kernel, rows_per_block=eb // N_KEYS),
        grid=(T // tb, E // eb),
        in_specs=[pl.BlockSpec((tb, D), lambda i, j: (i, 0)),
                  pl.BlockSpec((eb, D), lambda i, j: (j, 0)),
                  pl.BlockSpec((eb, D), lambda i, j: (j, 0)),
                  route_spec, route_spec, route_spec, route_spec,
                  pl.BlockSpec((tb, D), lambda i, j: (i, 0)),
                  pl.BlockSpec((None, 1, D), lambda i, j: (i // blocks_per_seq, 0, 0)),
                  pl.BlockSpec((1, D), lambda i, j: (0, 0))],
        out_specs=pl.BlockSpec((tb, D), lambda i, j: (i, 0)),
        out_shape=jax.ShapeDtypeStruct((T, D), F32),
        scratch_shapes=[pltpu.VMEM((eb, tb), F32), pltpu.VMEM((eb, tb), BF16), pltpu.VMEM((tb, D), F32)],
        compiler_params=_cparams(("parallel", "arbitrary")),
        name="peer_dense",
    )(h2, u, v, b_t, beta_t, ea_t, eb_t, x1, gate2, g_post)


def _pick(n, prefs):
    for p in prefs:
        if n % p == 0:
            return p
    raise ValueError(f"no supported tile for extent {n}")


def kernel(x, c, w_ada, b_ada, g_pre_mix, g_post_mix, g_pre_ffn, g_post_ffn, w_in, b_if, conv_w, conv_b,
           m_head_gain, lam_q1, lam_k1, lam_q2, lam_k2, d_head_gain, w_br_m, w_br_d, w_out,
           w_query, sub_keys, expert_u, expert_v):
    B, S, D = x.shape
    depth = w_ada.shape[0]
    assert depth == 1 and B <= 8
    m_width = M_HEADS * M_HEAD_DIM
    gate_off = 4 * m_width
    n_gate = 2 * M_HEADS

    for l in range(depth):
        c8 = jnp.pad(c, ((0, 8 - B), (0, 0)))
        ada = _ada(c8, w_ada[l], b_ada[l][None, :])[:B]
        shift1, scale1, gate1, shift2, scale2, gate2 = [a[:, None, :] for a in jnp.split(ada, ADA_PARTS, axis=-1)]

        w_in_l = w_in[l]
        w_main = jnp.concatenate([w_in_l[:, :gate_off], w_in_l[:, gate_off + n_gate:]], axis=1).astype(BF16)
        w_if = jnp.pad(w_in_l[:, gate_off:gate_off + n_gate], ((0, 0), (0, 128 - n_gate))).astype(BF16)
        b_if_p = jnp.pad(b_if[l], (0, 128 - n_gate))[None, :]

        proj, gates = _inproj(x, g_pre_mix[l][None, :], scale1, shift1, w_main, w_if, b_if_p,
                              tm=_pick(S, (1024, 512, 256)), tn=_pick(w_main.shape[1], (2304, 1152, 1024)))
        gates_rows = jnp.transpose(gates[:, :, :n_gate], (0, 2, 1))[:, :, None, :]

        y_m = _mlstm(proj, gates_rows, conv_w[l], conv_b[l][None, :], m_head_gain[l][None, :],
                     L=_pick(S, (256,)))
        y_d = _attn(proj, lam_q1[l][None, :], lam_k1[l][None, :], lam_q2[l][None, :], lam_k2[l][None, :],
                    d_head_gain[l][None, :], tq=_pick(S, (512, 256)), tk=_pick(S, (512, 256)))

        x1, h2, qry = _post(x, y_m, y_d, proj, gate1, scale2, shift2, g_post_mix[l][None, :],
                            g_pre_ffn[l][None, :], w_br_m[l].astype(BF16), w_br_d[l].astype(BF16),
                            w_out[l].astype(BF16), w_query[l].astype(BF16), tm=_pick(S, (512, 256)))

        T = B * S
        b_t, beta_t, ea_t, eb_t = _route(qry.reshape(T, -1), sub_keys[l].astype(BF16), tb=_pick(T, (1024, 512)))
        x = _peer(h2.reshape(T, D), expert_u[l].astype(BF16), expert_v[l].astype(BF16), b_t, beta_t, ea_t, eb_t,
                  x1.reshape(T, D), gate2, g_post_ffn[l][None, :], seq_len=S,
                  tb=_pick(S, (512, 256)), eb=2048).reshape(B, S, D)
    return x
```

```python
import functools
import math

import jax
import jax.numpy as jnp
from jax import lax
from jax.experimental import pallas as pl
from jax.experimental.pallas import tpu as pltpu

F32 = jnp.float32
BF16 = jnp.bfloat16
HIGHEST = lax.Precision.HIGHEST

EPS = 1e-6
ADA_PARTS = 6
CHUNK = 64
M_HEADS = 4
M_HEAD_DIM = 256
CONV_WIDTH = 4
D_HEADS = 4
D_QK_DIM = 128
D_V_DIM = 256
P_HEADS = 8
N_KEYS = 128
P_TOPK = 16
P_HALF = 128
LAM_INIT = 0.8 - 0.6 * math.exp(-0.3 * 0)

V7X_VMEM_LIMIT_BYTES = 56 * 1024 * 1024

NT_DIMS = (((1,), (1,)), ((), ()))
TN_DIMS = (((0,), (0,)), ((), ()))


def _cparams(semantics):
    return pltpu.CompilerParams(dimension_semantics=semantics, vmem_limit_bytes=V7X_VMEM_LIMIT_BYTES)


def _rms(x, gain):
    return x * lax.rsqrt(jnp.mean(x * x, axis=-1, keepdims=True) + EPS) * gain


def _ada_kernel(c_ref, w_ref, b_ref, o_ref):
    c = c_ref[...]
    sc = c * jax.nn.sigmoid(c)
    o_ref[...] = jnp.dot(sc.astype(BF16), w_ref[...].astype(BF16), preferred_element_type=F32) + b_ref[...]


def _ada(c8, w_ada, b_ada):
    d = c8.shape[1]
    n = w_ada.shape[1]
    tn = 1024
    return pl.pallas_call(
        _ada_kernel,
        grid=(n // tn,),
        in_specs=[pl.BlockSpec((8, d), lambda j: (0, 0)),
                  pl.BlockSpec((d, tn), lambda j: (0, j)),
                  pl.BlockSpec((1, tn), lambda j: (0, j))],
        out_specs=pl.BlockSpec((8, tn), lambda j: (0, j)),
        out_shape=jax.ShapeDtypeStruct((8, n), F32),
        compiler_params=_cparams(("arbitrary",)),
        name="ada",
    )(c8, w_ada, b_ada)


def _inproj_kernel(x_ref, g_ref, sc_ref, sh_ref, w_ref, wif_ref, bif_ref, o_ref, oif_ref, h_scr):
    @pl.when(pl.program_id(2) == 0)
    def _():
        h = _rms(x_ref[...], g_ref[...]) * (1.0 + sc_ref[...]) + sh_ref[...]
        hb = h.astype(BF16)
        h_scr[...] = hb
        oif_ref[...] = jnp.dot(hb, wif_ref[...], preferred_element_type=F32) + bif_ref[...]

    o_ref[...] = jnp.dot(h_scr[...], w_ref[...], preferred_element_type=F32).astype(o_ref.dtype)


def _inproj(x, gain, scale, shift, w_main, w_if, b_if, tm, tn):
    B, S, D = x.shape
    n = w_main.shape[1]
    return pl.pallas_call(
        _inproj_kernel,
        grid=(B, S // tm, n // tn),
        in_specs=[pl.BlockSpec((None, tm, D), lambda b, i, j: (b, i, 0)),
                  pl.BlockSpec((1, D), lambda b, i, j: (0, 0)),
                  pl.BlockSpec((None, 1, D), lambda b, i, j: (b, 0, 0)),
                  pl.BlockSpec((None, 1, D), lambda b, i, j: (b, 0, 0)),
                  pl.BlockSpec((D, tn), lambda b, i, j: (0, j)),
                  pl.BlockSpec((D, 128), lambda b, i, j: (0, 0)),
                  pl.BlockSpec((1, 128), lambda b, i, j: (0, 0))],
        out_specs=[pl.BlockSpec((None, tm, tn), lambda b, i, j: (b, i, j)),
                   pl.BlockSpec((None, tm, 128), lambda b, i, j: (b, i, 0))],
        out_shape=[jax.ShapeDtypeStruct((B, S, n), BF16),
                   jax.ShapeDtypeStruct((B, S, 128), F32)],
        scratch_shapes=[pltpu.VMEM((tm, D), BF16)],
        compiler_params=_cparams(("parallel", "parallel", "arbitrary")),
        name="inproj",
    )(x, gain, scale, shift, w_main, w_if, b_if)


def _mlstm_kernel(q_ref, k_ref, v_ref, mo_ref, cwq_ref, cwk_ref, cbq_ref, cbk_ref, gi_ref, gf_ref, gain_ref,
                  y_ref, ct_scr, n_scr, m_scr, qtail, ktail, xext, *, L):
    @pl.when(pl.program_id(2) == 0)
    def _():
        ct_scr[...] = jnp.zeros_like(ct_scr)
        n_scr[...] = jnp.zeros_like(n_scr)
        m_scr[...] = jnp.zeros_like(m_scr)
        qtail[...] = jnp.zeros_like(qtail)
        ktail[...] = jnp.zeros_like(ktail)

    def conv_silu(x_ref, tail, cw_ref, cb_ref):
        x = x_ref[...].astype(F32)
        xext[0:8, :] = tail[...]
        xext[8:, :] = x
        acc = jnp.broadcast_to(cb_ref[...], x.shape)
        for j in range(CONV_WIDTH):
            acc = acc + cw_ref[j:j + 1, :] * xext[5 + j:5 + j + L, :]
        tail[...] = x[L - 8:, :]
        return acc * jax.nn.sigmoid(acc)

    q = conv_silu(q_ref, qtail, cwq_ref, cbq_ref)
    k = conv_silu(k_ref, ktail, cwk_ref, cbk_ref) * (M_HEAD_DIM ** -0.5)
    v = v_ref[...]
    qb = q.astype(BF16)
    kb = k.astype(BF16)

    li_row = gi_ref[...]
    fp = gf_ref[...]
    lf_row = jnp.minimum(fp, 0.0) - jnp.log1p(jnp.exp(-jnp.abs(fp)))

    ti = lax.broadcasted_iota(jnp.int32, (L, L), 0)
    si = lax.broadcasted_iota(jnp.int32, (L, L), 1)
    causal = si <= ti
    tril = causal.astype(BF16)
    eye = (si == ti).astype(BF16)

    def bf16_pieces(x):
        hi = x.astype(BF16).astype(F32)
        mid = (x - hi).astype(BF16).astype(F32)
        lo = (x - hi - mid).astype(BF16).astype(F32)
        return hi, mid, lo

    pieces = bf16_pieces(lf_row) + bf16_pieces(li_row)
    rid = lax.broadcasted_iota(jnp.int32, (8, L), 0)
    rows = jnp.zeros((8, L), F32)
    for idx, piece in enumerate(pieces):
        rows = jnp.where(rid == idx, piece, rows)
    rows = rows.astype(BF16)
    cum_rows = lax.dot_general(rows, tril, NT_DIMS, preferred_element_type=F32)
    cum_cols = lax.dot_general(tril, rows, NT_DIMS, preferred_element_type=F32)
    raw_cols = lax.dot_general(eye, rows, NT_DIMS, preferred_element_type=F32)
    b_row = cum_rows[0:1, :] + cum_rows[1:2, :] + cum_rows[2:3, :]
    b_col = cum_cols[:, 0:1] + cum_cols[:, 1:2] + cum_cols[:, 2:3]
    li_col = raw_cols[:, 3:4] + raw_cols[:, 4:5] + raw_cols[:, 5:6]

    m_prev = m_scr[0:1, 0:1]
    a_col = b_col + m_prev
    dmat = jnp.where(causal, b_col - b_row + li_row, -jnp.inf)
    m_t = jnp.maximum(a_col, jnp.max(dmat, axis=1, keepdims=True))
    p = jnp.exp(dmat - m_t)
    s = lax.dot_general(qb, kb, NT_DIMS, preferred_element_type=F32) * p
    w_inter = jnp.exp(a_col - m_t)
    ct = ct_scr[...]
    num = (jnp.dot(s.astype(BF16), v, preferred_element_type=F32)
           + w_inter * jnp.dot(qb, ct.astype(BF16), preferred_element_type=F32))
    n_row = n_scr[...]
    den = jnp.sum(s, axis=1, keepdims=True) + w_inter * jnp.sum(q * n_row, axis=1, keepdims=True)
    h = num / jnp.maximum(jnp.abs(den), jnp.exp(-m_t))

    m_new = m_t[L - 1:L, :]
    b_last = b_col[L - 1:L, :]
    g_prev = jnp.exp(b_last + m_prev - m_new)
    gs_col = jnp.exp(b_last - b_col + li_col - m_new)
    gv = (gs_col * v.astype(F32)).astype(BF16)
    ct_scr[...] = g_prev * ct + lax.dot_general(kb, gv, TN_DIMS, preferred_element_type=F32)
    n_scr[...] = g_prev * n_row + jnp.sum(gs_col * k, axis=0, keepdims=True)
    m_scr[...] = jnp.broadcast_to(m_new, m_scr.shape)

    y = _rms(h, gain_ref[...]) * jax.nn.sigmoid(mo_ref[...].astype(F32))
    y_ref[...] = y.astype(y_ref.dtype)


def _mlstm(proj, gates_rows, conv_w, conv_b, head_gain, L):
    B, S, _ = proj.shape
    H, Dh = M_HEADS, M_HEAD_DIM
    col = lambda off: (lambda b, h, c: (b, c, off + h))
    return pl.pallas_call(
        functools.partial(_mlstm_kernel, L=L),
        grid=(B, H, S // L),
        in_specs=[pl.BlockSpec((None, L, Dh), col(0)),
                  pl.BlockSpec((None, L, Dh), col(H)),
                  pl.BlockSpec((None, L, Dh), col(2 * H)),
                  pl.BlockSpec((None, L, Dh), col(3 * H)),
                  pl.BlockSpec((CONV_WIDTH, Dh), lambda b, h, c: (0, h)),
                  pl.BlockSpec((CONV_WIDTH, Dh), lambda b, h, c: (0, H + h)),
                  pl.BlockSpec((1, Dh), lambda b, h, c: (0, h)),
                  pl.BlockSpec((1, Dh), lambda b, h, c: (0, H + h)),
                  pl.BlockSpec((None, None, 1, L), lambda b, h, c: (b, h, 0, c)),
                  pl.BlockSpec((None, None, 1, L), lambda b, h, c: (b, H + h, 0, c)),
                  pl.BlockSpec((1, Dh), lambda b, h, c: (0, h))],
        out_specs=pl.BlockSpec((None, L, Dh), lambda b, h, c: (b, c, h)),
        out_shape=jax.ShapeDtypeStruct((B, S, H * Dh), BF16),
        scratch_shapes=[pltpu.VMEM((Dh, Dh), F32), pltpu.VMEM((1, Dh), F32), pltpu.VMEM((8, 128), F32),
                        pltpu.VMEM((8, Dh), F32), pltpu.VMEM((8, Dh), F32), pltpu.VMEM((L + 8, Dh), F32)],
        compiler_params=_cparams(("parallel", "parallel", "arbitrary")),
        name="mlstm",
    )(proj, proj, proj, proj, conv_w, conv_w, conv_b, conv_b, gates_rows, gates_rows, head_gain)


def _attn_kernel(q_ref, k_ref, v_ref, lq1_ref, lk1_ref, lq2_ref, lk2_ref, gain_ref, o_ref,
                 m_scr, l_scr, acc_scr, *, tq, tk):
    qi = pl.program_id(2)
    c2 = (D_QK_DIM ** -0.5) * math.log2(math.e)

    m_scr[...] = jnp.full_like(m_scr, -jnp.inf)
    l_scr[...] = jnp.zeros_like(l_scr)
    acc_scr[...] = jnp.zeros_like(acc_scr)

    def update(k0, key_offset):
        v = v_ref[pl.ds(k0, tk), :]
        for c in range(2):
            q = q_ref[:, c * D_QK_DIM:(c + 1) * D_QK_DIM]
            k = k_ref[pl.ds(k0, tk), c * D_QK_DIM:(c + 1) * D_QK_DIM]
            s = lax.dot_general(q, k, NT_DIMS, preferred_element_type=F32)
            if key_offset is not None:
                tch = lax.broadcasted_iota(jnp.int32, s.shape, 0) // CHUNK
                sch = (lax.broadcasted_iota(jnp.int32, s.shape, 1) + key_offset) // CHUNK
                s = jnp.where(sch <= tch, s, -jnp.inf)
            m_old = m_scr[c]
            m_new = jnp.maximum(m_old, jnp.max(s, axis=1, keepdims=True))
            alpha = jnp.exp2((m_old - m_new) * c2)
            p = jnp.exp2((s - jnp.concatenate([m_new] * (tk // 128), axis=1)) * c2)
            p_lanes = p[:, 0:128]
            for g in range(1, tk // 128):
                p_lanes = p_lanes + p[:, g * 128:(g + 1) * 128]
            l_scr[c] = alpha * l_scr[c] + p_lanes
            acc_scr[c] = (jnp.concatenate([alpha] * (D_V_DIM // 128), axis=1) * acc_scr[c]
                          + jnp.dot(p.astype(BF16), v, preferred_element_type=F32))
            m_scr[c] = m_new

    def full_chunk(j, carry):
        update(pl.multiple_of(j * tk, tk), None)
        return carry

    lax.fori_loop(0, qi * (tq // tk), full_chunk, 0)
    for d in range(tq // tk):
        update(pl.multiple_of(qi * tq + d * tk, tk), d * tk)

    lam = (jnp.exp(jnp.sum(lq1_ref[...] * lk1_ref[...], axis=1, keepdims=True))
           - jnp.exp(jnp.sum(lq2_ref[...] * lk2_ref[...], axis=1, keepdims=True)) + LAM_INIT)
    l0 = jnp.sum(l_scr[0], axis=1, keepdims=True)
    l1 = jnp.sum(l_scr[1], axis=1, keepdims=True)
    o = acc_scr[0] / l0 - lam * (acc_scr[1] / l1)
    o_ref[...] = (_rms(o, gain_ref[...]) * (1.0 - LAM_INIT)).astype(o_ref.dtype)


def _attn(proj, lam_q1, lam_k1, lam_q2, lam_k2, head_gain, tq, tk):
    B, S, _ = proj.shape
    H = D_HEADS
    blk0 = 4 * M_HEADS
    lam_spec = pl.BlockSpec((1, D_QK_DIM), lambda b, h, qi: (0, 0))
    return pl.pallas_call(
        functools.partial(_attn_kernel, tq=tq, tk=tk),
        grid=(B, H, S // tq),
        in_specs=[pl.BlockSpec((None, tq, 256), lambda b, h, qi: (b, qi, blk0 + h)),
                  pl.BlockSpec((None, S, 256), lambda b, h, qi: (b, 0, blk0 + H + h)),
                  pl.BlockSpec((None, S, 256), lambda b, h, qi: (b, 0, blk0 + 2 * H + h)),
                  lam_spec, lam_spec, lam_spec, lam_spec,
                  pl.BlockSpec((1, D_V_DIM), lambda b, h, qi: (0, h))],
        out_specs=pl.BlockSpec((None, tq, D_V_DIM), lambda b, h, qi: (b, qi, h)),
        out_shape=jax.ShapeDtypeStruct((B, S, H * D_V_DIM), BF16),
        scratch_shapes=[pltpu.VMEM((2, tq, 128), F32), pltpu.VMEM((2, tq, 128), F32),
                        pltpu.VMEM((2, tq, D_V_DIM), F32)],
        compiler_params=_cparams(("parallel", "parallel", "arbitrary")),
        name="diff_attn",
    )(proj, proj, proj, lam_q1, lam_k1, lam_q2, lam_k2, head_gain)


def _post_kernel(x_ref, ym_ref, yd_ref, gm_ref, gd_ref, gate1_ref, sc2_ref, sh2_ref, gpost_ref, gpre_ref,
                 wbm_ref, wbd_ref, wo_ref, wq_ref, x1_ref, h2_ref, qry_ref):
    bm = jnp.dot(ym_ref[...], wbm_ref[...], preferred_element_type=F32)
    bd = jnp.dot(yd_ref[...], wbd_ref[...], preferred_element_type=F32)
    merged = jax.nn.sigmoid(gm_ref[...].astype(F32)) * bm + jax.nn.sigmoid(gd_ref[...].astype(F32)) * bd
    y = jnp.dot(merged.astype(BF16), wo_ref[...], preferred_element_type=F32)
    x1 = x_ref[...] + gate1_ref[...] * _rms(y, gpost_ref[...])
    x1_ref[...] = x1
    h2 = (_rms(x1, gpre_ref[...]) * (1.0 + sc2_ref[...]) + sh2_ref[...]).astype(BF16)
    h2_ref[...] = h2
    qry_ref[...] = jnp.dot(h2, wq_ref[...], preferred_element_type=F32)


def _post(x, y_m, y_d, proj, gate1, scale2, shift2, g_post, g_pre, w_br_m, w_br_d, w_out, w_query, tm):
    B, S, D = x.shape
    nq = w_query.shape[1]
    row = pl.BlockSpec((None, tm, D), lambda b, i: (b, i, 0))
    per_b = pl.BlockSpec((None, 1, D), lambda b, i: (b, 0, 0))
    vec = pl.BlockSpec((1, D), lambda b, i: (0, 0))
    wsq = pl.BlockSpec((D, D), lambda b, i: (0, 0))
    return pl.pallas_call(
        _post_kernel,
        grid=(B, S // tm),
        in_specs=[row, row, row,
                  pl.BlockSpec((None, tm, D), lambda b, i: (b, i, 7)),
                  pl.BlockSpec((None, tm, D), lambda b, i: (b, i, 8)),
                  per_b, per_b, per_b, vec, vec, wsq, wsq, wsq,
                  pl.BlockSpec((D, nq), lambda b, i: (0, 0))],
        out_specs=[row, row, pl.BlockSpec((None, tm, nq), lambda b, i: (b, i, 0))],
        out_shape=[jax.ShapeDtypeStruct((B, S, D), F32),
                   jax.ShapeDtypeStruct((B, S, D), BF16),
                   jax.ShapeDtypeStruct((B, S, nq), F32)],
        compiler_params=_cparams(("parallel", "parallel")),
        name="post_mix",
    )(x, y_m, y_d, proj, proj, gate1, scale2, shift2, g_post, g_pre, w_br_m, w_br_d, w_out, w_query)


SUBLANES = 8


def _sort16_pairs():
    n, out, p = P_TOPK, [], 1
    while p < n:
        k = p
        while k >= 1:
            for j in range(k % p, n - k, 2 * k):
                for i in range(min(k, n - j - k)):
                    if (i + j) // (2 * p) == (i + j + k) // (2 * p):
                        out.append((i + j, i + j + k))
            k //= 2
        p *= 2
    return out


def _exchange(vs, i, j):
    hi, lo = jnp.maximum(vs[i], vs[j]), jnp.minimum(vs[i], vs[j])
    vs[i], vs[j] = hi, lo


def _merge_over_sublanes(vs):
    for shift in (4, 2, 1):
        other = [pltpu.roll(v, shift, axis=0) for v in vs]
        vs = [jnp.maximum(vs[k], other[P_TOPK - 1 - k]) for k in range(P_TOPK)]
        d = P_TOPK // 2
        while d >= 1:
            for k in range(P_TOPK):
                if k & d == 0:
                    _exchange(vs, k, k + d)
            d //= 2
    return vs


def _top16_of_keys(slices):
    vs = list(slices)
    for i, j in _sort16_pairs():
        _exchange(vs, i, j)
    return _merge_over_sublanes(vs)


def _route_kernel(q_ref, keys_ref, bt_ref, beta_ref, ea_ref, eb_ref):
    n_groups = N_KEYS // SUBLANES
    raw = []
    for c in range(2):
        q = q_ref[:, c * P_HALF:(c + 1) * P_HALF].astype(keys_ref.dtype)
        sc = lax.dot_general(keys_ref[c], q, NT_DIMS, preferred_element_type=F32)
        raw.append([sc[g * SUBLANES:(g + 1) * SUBLANES, :] for g in range(n_groups)])
    a_rows, b_rows = raw
    a_top = _top16_of_keys(a_rows)
    b_top = _top16_of_keys(b_rows)

    sub = lax.broadcasted_iota(jnp.int32, a_top[0].shape, 0)
    b_lo, b_hi = b_top[0], b_top[SUBLANES]
    for qq in range(1, SUBLANES):
        b_lo = jnp.where(sub == qq, b_top[qq], b_lo)
        b_hi = jnp.where(sub == qq, b_top[SUBLANES + qq], b_hi)
    cand = []
    for p in range(P_TOPK):
        n_valid = P_TOPK // (p + 1)
        cp = a_top[p] + b_lo
        cand.append(cp if n_valid >= SUBLANES else jnp.where(sub < n_valid, cp, -jnp.inf))
    extra = a_top[0] + b_hi
    for k in range(P_TOPK):
        cand[k], extra = jnp.maximum(cand[k], extra), jnp.minimum(cand[k], extra)
    best = _merge_over_sublanes(cand)

    tau = best[P_TOPK - 1]
    z = jnp.ones_like(tau)
    for k in range(1, P_TOPK):
        z = z + jnp.exp(best[k] - best[0])
    inv_z = 1.0 / z
    for g in range(n_groups):
        rows = pl.ds(g * SUBLANES, SUBLANES)
        beta = jnp.full_like(tau, jnp.inf)
        for qq in range(P_TOPK):
            beta = jnp.where(a_rows[g] + b_top[qq] >= tau, b_top[qq], beta)
        beta_ref[rows, :] = beta
        bt_ref[rows, :] = b_rows[g]
        ea_ref[rows, :] = jnp.exp(a_rows[g] - a_top[0])
        eb_ref[rows, :] = jnp.exp(b_rows[g] - b_top[0]) * inv_z


def _route(qry, keys, tb):
    T = qry.shape[0]
    out_spec = pl.BlockSpec((None, N_KEYS, tb), lambda i, h: (h, 0, i))
    out_shape = jax.ShapeDtypeStruct((P_HEADS, N_KEYS, T), F32)
    return pl.pallas_call(
        _route_kernel,
        grid=(T // tb, P_HEADS),
        in_specs=[pl.BlockSpec((tb, 2 * P_HALF), lambda i, h: (i, h)),
                  pl.BlockSpec((None, 2, N_KEYS, P_HALF), lambda i, h: (h, 0, 0, 0))],
        out_specs=[out_spec] * 4,
        out_shape=[out_shape] * 4,
        compiler_params=_cparams(("parallel", "parallel")),
        name="peer_route",
    )(qry, keys)


def _peer_kernel(ht_ref, u_ref, vt_ref, bt_ref, beta_ref, ea_ref, eb_ref, x1_ref, gate2_ref, gpost_ref, o_ref,
                 act_a, act_b, acc_scr, skew_scr, *, n_pairs):
    j = pl.program_id(1)
    pair = 2 * N_KEYS

    tb = acc_scr.shape[1]

    @pl.when(j == 0)
    def _():
        acc_scr[...] = jnp.zeros_like(acc_scr)
        for h in range(P_HEADS):
            skew_scr[2 * h, 0:N_KEYS, 0:tb] = bt_ref[h]
            skew_scr[2 * h + 1, 0:N_KEYS, 0:tb] = eb_ref[h]

    def pre_activations(p):
        rows = pl.ds(pl.multiple_of(p * pair, pair), pair)
        return jnp.dot(u_ref[rows, :], ht_ref[...], preferred_element_type=F32)

    def gated(act_ref, p):
        parts = []
        for g in range(2):
            key_row = (j * n_pairs + p) * 2 + g
            gate = None
            for h in range(P_HEADS):
                picked = skew_scr[2 * h, 0:N_KEYS, 0:tb] >= beta_ref[h, pl.ds(key_row, 1), :]
                term = ea_ref[h, pl.ds(key_row, 1), :] * jnp.where(picked, skew_scr[2 * h + 1, 0:N_KEYS, 0:tb], 0.0)
                gate = term if gate is None else gate + term
            a = act_ref[g * N_KEYS:(g + 1) * N_KEYS, :]
            gelu = a * (0.5 + 0.5 * lax.erf(a * (2.0 ** -0.5)))
            parts.append((gelu * gate).astype(BF16))
        return jnp.concatenate(parts, axis=0)

    def accumulate(p, w):
        cols = pl.ds(pl.multiple_of(p * pair, pair), pair)
        acc_scr[...] += jnp.dot(vt_ref[:, cols], w, preferred_element_type=F32)

    act_a[...] = pre_activations(0)

    def two_pairs(m, carry):
        p0 = 2 * m
        act_b[...] = pre_activations(p0 + 1)
        accumulate(p0, gated(act_a, p0))
        act_a[...] = pre_activations(jnp.minimum(p0 + 2, n_pairs - 1))
        accumulate(p0 + 1, gated(act_b, p0 + 1))
        return carry

    lax.fori_loop(0, n_pairs // 2, two_pairs, 0)

    @pl.when(j == pl.num_programs(1) - 1)
    def _():
        o_ref[...] = x1_ref[...] + gate2_ref[...] * _rms(acc_scr[...].T, gpost_ref[...])


def _peer(h2_t, u, v_t, b_t, beta_t, ea_t, eb_t, x1, gate2, g_post, seq_len, tb, eb):
    D, T = h2_t.shape
    E = u.shape[0]
    blocks_per_seq = seq_len // tb
    pair = 2 * N_KEYS
    route_spec = pl.BlockSpec((P_HEADS, N_KEYS, tb), lambda i, j: (0, 0, i))
    return pl.pallas_call(
        functools.partial(_peer_kernel, n_pairs=eb // pair),
        grid=(T // tb, E // eb),
        in_specs=[pl.BlockSpec((D, tb), lambda i, j: (0, i)),
                  pl.BlockSpec((eb, D), lambda i, j: (j, 0)),
                  pl.BlockSpec((D, eb), lambda i, j: (0, j)),
                  route_spec, route_spec, route_spec, route_spec,
                  pl.BlockSpec((tb, D), lambda i, j: (i, 0)),
                  pl.BlockSpec((None, 1, D), lambda i, j: (i // blocks_per_seq, 0, 0)),
                  pl.BlockSpec((1, D), lambda i, j: (0, 0))],
        out_specs=pl.BlockSpec((tb, D), lambda i, j: (i, 0)),
        out_shape=jax.ShapeDtypeStruct((T, D), F32),
        scratch_shapes=[pltpu.VMEM((pair, tb), F32), pltpu.VMEM((pair, tb), F32), pltpu.VMEM((D, tb), F32),
                        pltpu.VMEM((2 * P_HEADS, N_KEYS + SUBLANES, tb + 128), F32)],
        compiler_params=_cparams(("parallel", "arbitrary")),
        name="peer_dense",
    )(h2_t, u, v_t, b_t, beta_t, ea_t, eb_t, x1, gate2, g_post)


def _pick(n, prefs):
    for p in prefs:
        if n % p == 0:
            return p
    raise ValueError(f"no supported tile for extent {n}")


def kernel(x, c, w_ada, b_ada, g_pre_mix, g_post_mix, g_pre_ffn, g_post_ffn, w_in, b_if, conv_w, conv_b,
           m_head_gain, lam_q1, lam_k1, lam_q2, lam_k2, d_head_gain, w_br_m, w_br_d, w_out,
           w_query, sub_keys, expert_u, expert_v):
    B, S, D = x.shape
    depth = w_ada.shape[0]
    assert depth == 1 and B <= 8
    m_width = M_HEADS * M_HEAD_DIM
    gate_off = 4 * m_width
    n_gate = 2 * M_HEADS

    for l in range(depth):
        c8 = jnp.pad(c, ((0, 8 - B), (0, 0)))
        ada = _ada(c8, w_ada[l], b_ada[l][None, :])[:B]
        shift1, scale1, gate1, shift2, scale2, gate2 = [a[:, None, :] for a in jnp.split(ada, ADA_PARTS, axis=-1)]

        w_in_l = w_in[l]
        w_main = jnp.concatenate([w_in_l[:, :gate_off], w_in_l[:, gate_off + n_gate:]], axis=1).astype(BF16)
        w_if = jnp.pad(w_in_l[:, gate_off:gate_off + n_gate], ((0, 0), (0, 128 - n_gate))).astype(BF16)
        b_if_p = jnp.pad(b_if[l], (0, 128 - n_gate))[None, :]

        proj, gates = _inproj(x, g_pre_mix[l][None, :], scale1, shift1, w_main, w_if, b_if_p,
                              tm=_pick(S, (1024, 512, 256)), tn=_pick(w_main.shape[1], (2304, 1152, 1024)))
        gates_rows = jnp.transpose(gates[:, :, :n_gate], (0, 2, 1))[:, :, None, :]

        y_m = _mlstm(proj, gates_rows, conv_w[l], conv_b[l][None, :], m_head_gain[l][None, :],
                     L=_pick(S, (256,)))
        y_d = _attn(proj, lam_q1[l][None, :], lam_k1[l][None, :], lam_q2[l][None, :], lam_k2[l][None, :],
                    d_head_gain[l][None, :], tq=_pick(S, (512, 256)), tk=_pick(S, (512, 256)))

        x1, h2, qry = _post(x, y_m, y_d, proj, gate1, scale2, shift2, g_post_mix[l][None, :],
                            g_pre_ffn[l][None, :], w_br_m[l].astype(BF16), w_br_d[l].astype(BF16),
                            w_out[l].astype(BF16), w_query[l].astype(BF16), tm=_pick(S, (512, 256)))

        T = B * S
        b_t, beta_t, ea_t, eb_t = _route(qry.reshape(T, -1), sub_keys[l].astype(BF16), tb=_pick(T, (1024, 512)))
        x = _peer(h2.reshape(T, D).T, expert_u[l].astype(BF16), expert_v[l].astype(BF16).T, b_t, beta_t, ea_t, eb_t,
                  x1.reshape(T, D), gate2, g_post_ffn[l][None, :], seq_len=S,
                  tb=_pick(S, (512, 256)), eb=2048).reshape(B, S, D)
    return x
```

```python
import functools
import math

import jax
import jax.numpy as jnp
from jax import lax
from jax.experimental import pallas as pl
from jax.experimental.pallas import tpu as pltpu

F32 = jnp.float32
BF16 = jnp.bfloat16
HIGHEST = lax.Precision.HIGHEST

EPS = 1e-6
ADA_PARTS = 6
CHUNK = 64
M_HEADS = 4
M_HEAD_DIM = 256
CONV_WIDTH = 4
D_HEADS = 4
D_QK_DIM = 128
D_V_DIM = 256
P_HEADS = 8
N_KEYS = 128
P_TOPK = 16
P_HALF = 128
LAM_INIT = 0.8 - 0.6 * math.exp(-0.3 * 0)

V7X_VMEM_LIMIT_BYTES = 56 * 1024 * 1024

NT_DIMS = (((1,), (1,)), ((), ()))
TN_DIMS = (((0,), (0,)), ((), ()))


def _cparams(semantics):
    return pltpu.CompilerParams(dimension_semantics=semantics, vmem_limit_bytes=V7X_VMEM_LIMIT_BYTES)


def _rms(x, gain):
    return x * lax.rsqrt(jnp.mean(x * x, axis=-1, keepdims=True) + EPS) * gain


def _ada_kernel(c_ref, w_ref, b_ref, o_ref):
    c = c_ref[...]
    sc = c * jax.nn.sigmoid(c)
    o_ref[...] = jnp.dot(sc.astype(BF16), w_ref[...].astype(BF16), preferred_element_type=F32) + b_ref[...]


def _ada(c8, w_ada, b_ada):
    d = c8.shape[1]
    n = w_ada.shape[1]
    tn = 1024
    return pl.pallas_call(
        _ada_kernel,
        grid=(n // tn,),
        in_specs=[pl.BlockSpec((8, d), lambda j: (0, 0)),
                  pl.BlockSpec((d, tn), lambda j: (0, j)),
                  pl.BlockSpec((1, tn), lambda j: (0, j))],
        out_specs=pl.BlockSpec((8, tn), lambda j: (0, j)),
        out_shape=jax.ShapeDtypeStruct((8, n), F32),
        compiler_params=_cparams(("arbitrary",)),
        name="ada",
    )(c8, w_ada, b_ada)


def _inproj_kernel(x_ref, g_ref, sc_ref, sh_ref, w_ref, wif_ref, bif_ref, o_ref, oif_ref, h_scr):
    @pl.when(pl.program_id(2) == 0)
    def _():
        h = _rms(x_ref[...], g_ref[...]) * (1.0 + sc_ref[...]) + sh_ref[...]
        hb = h.astype(BF16)
        h_scr[...] = hb
        oif_ref[...] = jnp.dot(hb, wif_ref[...], preferred_element_type=F32) + bif_ref[...]

    o_ref[...] = jnp.dot(h_scr[...], w_ref[...], preferred_element_type=F32).astype(o_ref.dtype)


def _inproj(x, gain, scale, shift, w_main, w_if, b_if, tm, tn):
    B, S, D = x.shape
    n = w_main.shape[1]
    return pl.pallas_call(
        _inproj_kernel,
        grid=(B, S // tm, n // tn),
        in_specs=[pl.BlockSpec((None, tm, D), lambda b, i, j: (b, i, 0)),
                  pl.BlockSpec((1, D), lambda b, i, j: (0, 0)),
                  pl.BlockSpec((None, 1, D), lambda b, i, j: (b, 0, 0)),
                  pl.BlockSpec((None, 1, D), lambda b, i, j: (b, 0, 0)),
                  pl.BlockSpec((D, tn), lambda b, i, j: (0, j)),
                  pl.BlockSpec((D, 128), lambda b, i, j: (0, 0)),
                  pl.BlockSpec((1, 128), lambda b, i, j: (0, 0))],
        out_specs=[pl.BlockSpec((None, tm, tn), lambda b, i, j: (b, i, j)),
                   pl.BlockSpec((None, tm, 128), lambda b, i, j: (b, i, 0))],
        out_shape=[jax.ShapeDtypeStruct((B, S, n), BF16),
                   jax.ShapeDtypeStruct((B, S, 128), F32)],
        scratch_shapes=[pltpu.VMEM((tm, D), BF16)],
        compiler_params=_cparams(("parallel", "parallel", "arbitrary")),
        name="inproj",
    )(x, gain, scale, shift, w_main, w_if, b_if)


def _mlstm_kernel(q_ref, k_ref, v_ref, mo_ref, cwq_ref, cwk_ref, cbq_ref, cbk_ref, gi_ref, gf_ref, gain_ref,
                  y_ref, ct_scr, n_scr, m_scr, qtail, ktail, xext, *, L):
    @pl.when(pl.program_id(2) == 0)
    def _():
        ct_scr[...] = jnp.zeros_like(ct_scr)
        n_scr[...] = jnp.zeros_like(n_scr)
        m_scr[...] = jnp.zeros_like(m_scr)
        qtail[...] = jnp.zeros_like(qtail)
        ktail[...] = jnp.zeros_like(ktail)

    def conv_silu(x_ref, tail, cw_ref, cb_ref):
        x = x_ref[...].astype(F32)
        xext[0:8, :] = tail[...]
        xext[8:, :] = x
        acc = jnp.broadcast_to(cb_ref[...], x.shape)
        for j in range(CONV_WIDTH):
            acc = acc + cw_ref[j:j + 1, :] * xext[5 + j:5 + j + L, :]
        tail[...] = x[L - 8:, :]
        return acc * jax.nn.sigmoid(acc)

    q = conv_silu(q_ref, qtail, cwq_ref, cbq_ref)
    k = conv_silu(k_ref, ktail, cwk_ref, cbk_ref) * (M_HEAD_DIM ** -0.5)
    v = v_ref[...]
    qb = q.astype(BF16)
    kb = k.astype(BF16)

    li_row = gi_ref[...]
    fp = gf_ref[...]
    lf_row = jnp.minimum(fp, 0.0) - jnp.log1p(jnp.exp(-jnp.abs(fp)))

    ti = lax.broadcasted_iota(jnp.int32, (L, L), 0)
    si = lax.broadcasted_iota(jnp.int32, (L, L), 1)
    causal = si <= ti
    tril = causal.astype(BF16)
    eye = (si == ti).astype(BF16)

    def bf16_pieces(x):
        hi = x.astype(BF16).astype(F32)
        mid = (x - hi).astype(BF16).astype(F32)
        lo = (x - hi - mid).astype(BF16).astype(F32)
        return hi, mid, lo

    pieces = bf16_pieces(lf_row) + bf16_pieces(li_row)
    rid = lax.broadcasted_iota(jnp.int32, (8, L), 0)
    rows = jnp.zeros((8, L), F32)
    for idx, piece in enumerate(pieces):
        rows = jnp.where(rid == idx, piece, rows)
    rows = rows.astype(BF16)
    cum_rows = lax.dot_general(rows, tril, NT_DIMS, preferred_element_type=F32)
    cum_cols = lax.dot_general(tril, rows, NT_DIMS, preferred_element_type=F32)
    raw_cols = lax.dot_general(eye, rows, NT_DIMS, preferred_element_type=F32)
    b_row = cum_rows[0:1, :] + cum_rows[1:2, :] + cum_rows[2:3, :]
    b_col = cum_cols[:, 0:1] + cum_cols[:, 1:2] + cum_cols[:, 2:3]
    li_col = raw_cols[:, 3:4] + raw_cols[:, 4:5] + raw_cols[:, 5:6]

    m_prev = m_scr[0:1, 0:1]
    a_col = b_col + m_prev
    dmat = jnp.where(causal, b_col - b_row + li_row, -jnp.inf)
    m_t = jnp.maximum(a_col, jnp.max(dmat, axis=1, keepdims=True))
    p = jnp.exp(dmat - m_t)
    s = lax.dot_general(qb, kb, NT_DIMS, preferred_element_type=F32) * p
    w_inter = jnp.exp(a_col - m_t)
    ct = ct_scr[...]
    num = (jnp.dot(s.astype(BF16), v, preferred_element_type=F32)
           + w_inter * jnp.dot(qb, ct.astype(BF16), preferred_element_type=F32))
    n_row = n_scr[...]
    den = jnp.sum(s, axis=1, keepdims=True) + w_inter * jnp.sum(q * n_row, axis=1, keepdims=True)
    h = num / jnp.maximum(jnp.abs(den), jnp.exp(-m_t))

    m_new = m_t[L - 1:L, :]
    b_last = b_col[L - 1:L, :]
    g_prev = jnp.exp(b_last + m_prev - m_new)
    gs_col = jnp.exp(b_last - b_col + li_col - m_new)
    gv = (gs_col * v.astype(F32)).astype(BF16)
    ct_scr[...] = g_prev * ct + lax.dot_general(kb, gv, TN_DIMS, preferred_element_type=F32)
    n_scr[...] = g_prev * n_row + jnp.sum(gs_col * k, axis=0, keepdims=True)
    m_scr[...] = jnp.broadcast_to(m_new, m_scr.shape)

    y = _rms(h, gain_ref[...]) * jax.nn.sigmoid(mo_ref[...].astype(F32))
    y_ref[...] = y.astype(y_ref.dtype)


def _mlstm(proj, gates_rows, conv_w, conv_b, head_gain, L):
    B, S, _ = proj.shape
    H, Dh = M_HEADS, M_HEAD_DIM
    col = lambda off: (lambda b, h, c: (b, c, off + h))
    return pl.pallas_call(
        functools.partial(_mlstm_kernel, L=L),
        grid=(B, H, S // L),
        in_specs=[pl.BlockSpec((None, L, Dh), col(0)),
                  pl.BlockSpec((None, L, Dh), col(H)),
                  pl.BlockSpec((None, L, Dh), col(2 * H)),
                  pl.BlockSpec((None, L, Dh), col(3 * H)),
                  pl.BlockSpec((CONV_WIDTH, Dh), lambda b, h, c: (0, h)),
                  pl.BlockSpec((CONV_WIDTH, Dh), lambda b, h, c: (0, H + h)),
                  pl.BlockSpec((1, Dh), lambda b, h, c: (0, h)),
                  pl.BlockSpec((1, Dh), lambda b, h, c: (0, H + h)),
                  pl.BlockSpec((None, None, 1, L), lambda b, h, c: (b, h, 0, c)),
                  pl.BlockSpec((None, None, 1, L), lambda b, h, c: (b, H + h, 0, c)),
                  pl.BlockSpec((1, Dh), lambda b, h, c: (0, h))],
        out_specs=pl.BlockSpec((None, L, Dh), lambda b, h, c: (b, c, h)),
        out_shape=jax.ShapeDtypeStruct((B, S, H * Dh), BF16),
        scratch_shapes=[pltpu.VMEM((Dh, Dh), F32), pltpu.VMEM((1, Dh), F32), pltpu.VMEM((8, 128), F32),
                        pltpu.VMEM((8, Dh), F32), pltpu.VMEM((8, Dh), F32), pltpu.VMEM((L + 8, Dh), F32)],
        compiler_params=_cparams(("parallel", "parallel", "arbitrary")),
        name="mlstm",
    )(proj, proj, proj, proj, conv_w, conv_w, conv_b, conv_b, gates_rows, gates_rows, head_gain)


def _attn_kernel(q_ref, k_ref, v_ref, lq1_ref, lk1_ref, lq2_ref, lk2_ref, gain_ref, o_ref,
                 m_scr, l_scr, acc_scr, s_a, s_b, *, tq, tk):
    assert tq == tk
    qi = pl.program_id(2)
    c2 = (D_QK_DIM ** -0.5) * math.log2(math.e)

    m_scr[...] = jnp.full_like(m_scr, -jnp.inf)
    l_scr[...] = jnp.zeros_like(l_scr)
    acc_scr[...] = jnp.zeros_like(acc_scr)

    def scores(dst, k0, diagonal):
        for c in range(2):
            q = q_ref[:, c * D_QK_DIM:(c + 1) * D_QK_DIM]
            k = k_ref[pl.ds(k0, tk), c * D_QK_DIM:(c + 1) * D_QK_DIM]
            s = lax.dot_general(q, k, NT_DIMS, preferred_element_type=F32)
            if diagonal:
                tch = lax.broadcasted_iota(jnp.int32, s.shape, 0) // CHUNK
                sch = lax.broadcasted_iota(jnp.int32, s.shape, 1) // CHUNK
                s = jnp.where(sch <= tch, s, -jnp.inf)
            dst[c] = s

    def fold(src, k0):
        v = v_ref[pl.ds(k0, tk), :]
        for c in range(2):
            s = src[c]
            m_old = m_scr[c]
            m_new = jnp.maximum(m_old, jnp.max(s, axis=1, keepdims=True))
            alpha = jnp.exp2((m_old - m_new) * c2)
            p = jnp.exp2((s - jnp.concatenate([m_new] * (tk // 128), axis=1)) * c2)
            p_lanes = p[:, 0:128]
            for g in range(1, tk // 128):
                p_lanes = p_lanes + p[:, g * 128:(g + 1) * 128]
            l_scr[c] = alpha * l_scr[c] + p_lanes
            acc_scr[c] = (jnp.concatenate([alpha] * (D_V_DIM // 128), axis=1) * acc_scr[c]
                          + jnp.dot(p.astype(BF16), v, preferred_element_type=F32))
            m_scr[c] = m_new

    n_full = qi
    q0 = pl.multiple_of(qi * tq, tq)
    scores(s_a, 0, False)

    def two_chunks(m, carry):
        k0 = pl.multiple_of(2 * m * tk, tk)
        scores(s_b, k0 + tk, False)
        fold(s_a, k0)
        scores(s_a, k0 + 2 * tk, False)
        fold(s_b, k0 + tk)
        return carry

    lax.fori_loop(0, n_full // 2, two_chunks, 0)

    @pl.when(n_full % 2 == 1)
    def _():
        scores(s_b, q0, True)
        fold(s_a, q0 - tk)
        fold(s_b, q0)

    @pl.when(n_full % 2 == 0)
    def _():
        scores(s_b, q0, True)
        fold(s_b, q0)

    lam = (jnp.exp(jnp.sum(lq1_ref[...] * lk1_ref[...], axis=1, keepdims=True))
           - jnp.exp(jnp.sum(lq2_ref[...] * lk2_ref[...], axis=1, keepdims=True)) + LAM_INIT)
    l0 = jnp.sum(l_scr[0], axis=1, keepdims=True)
    l1 = jnp.sum(l_scr[1], axis=1, keepdims=True)
    o = acc_scr[0] / l0 - lam * (acc_scr[1] / l1)
    o_ref[...] = (_rms(o, gain_ref[...]) * (1.0 - LAM_INIT)).astype(o_ref.dtype)


def _attn(proj, lam_q1, lam_k1, lam_q2, lam_k2, head_gain, tq, tk):
    B, S, _ = proj.shape
    H = D_HEADS
    blk0 = 4 * M_HEADS
    lam_spec = pl.BlockSpec((1, D_QK_DIM), lambda b, h, qi: (0, 0))
    return pl.pallas_call(
        functools.partial(_attn_kernel, tq=tq, tk=tk),
        grid=(B, H, S // tq),
        in_specs=[pl.BlockSpec((None, tq, 256), lambda b, h, qi: (b, qi, blk0 + h)),
                  pl.BlockSpec((None, S, 256), lambda b, h, qi: (b, 0, blk0 + H + h)),
                  pl.BlockSpec((None, S, 256), lambda b, h, qi: (b, 0, blk0 + 2 * H + h)),
                  lam_spec, lam_spec, lam_spec, lam_spec,
                  pl.BlockSpec((1, D_V_DIM), lambda b, h, qi: (0, h))],
        out_specs=pl.BlockSpec((None, tq, D_V_DIM), lambda b, h, qi: (b, qi, h)),
        out_shape=jax.ShapeDtypeStruct((B, S, H * D_V_DIM), BF16),
        scratch_shapes=[pltpu.VMEM((2, tq, 128), F32), pltpu.VMEM((2, tq, 128), F32),
                        pltpu.VMEM((2, tq, D_V_DIM), F32),
                        pltpu.VMEM((2, tq, tk), F32), pltpu.VMEM((2, tq, tk), F32)],
        compiler_params=_cparams(("parallel", "parallel", "arbitrary")),
        name="diff_attn",
    )(proj, proj, proj, lam_q1, lam_k1, lam_q2, lam_k2, head_gain)


def _post_kernel(x_ref, ym_ref, yd_ref, gm_ref, gd_ref, gate1_ref, sc2_ref, sh2_ref, gpost_ref, gpre_ref,
                 wbm_ref, wbd_ref, wo_ref, wq_ref, x1_ref, h2_ref, qry_ref):
    bm = jnp.dot(ym_ref[...], wbm_ref[...], preferred_element_type=F32)
    bd = jnp.dot(yd_ref[...], wbd_ref[...], preferred_element_type=F32)
    merged = jax.nn.sigmoid(gm_ref[...].astype(F32)) * bm + jax.nn.sigmoid(gd_ref[...].astype(F32)) * bd
    y = jnp.dot(merged.astype(BF16), wo_ref[...], preferred_element_type=F32)
    x1 = x_ref[...] + gate1_ref[...] * _rms(y, gpost_ref[...])
    x1_ref[...] = x1
    h2 = (_rms(x1, gpre_ref[...]) * (1.0 + sc2_ref[...]) + sh2_ref[...]).astype(BF16)
    h2_ref[...] = h2
    qry_ref[...] = jnp.dot(h2, wq_ref[...], preferred_element_type=F32)


def _post(x, y_m, y_d, proj, gate1, scale2, shift2, g_post, g_pre, w_br_m, w_br_d, w_out, w_query, tm):
    B, S, D = x.shape
    nq = w_query.shape[1]
    row = pl.BlockSpec((None, tm, D), lambda b, i: (b, i, 0))
    per_b = pl.BlockSpec((None, 1, D), lambda b, i: (b, 0, 0))
    vec = pl.BlockSpec((1, D), lambda b, i: (0, 0))
    wsq = pl.BlockSpec((D, D), lambda b, i: (0, 0))
    return pl.pallas_call(
        _post_kernel,
        grid=(B, S // tm),
        in_specs=[row, row, row,
                  pl.BlockSpec((None, tm, D), lambda b, i: (b, i, 7)),
                  pl.BlockSpec((None, tm, D), lambda b, i: (b, i, 8)),
                  per_b, per_b, per_b, vec, vec, wsq, wsq, wsq,
                  pl.BlockSpec((D, nq), lambda b, i: (0, 0))],
        out_specs=[row, row, pl.BlockSpec((None, tm, nq), lambda b, i: (b, i, 0))],
        out_shape=[jax.ShapeDtypeStruct((B, S, D), F32),
                   jax.ShapeDtypeStruct((B, S, D), BF16),
                   jax.ShapeDtypeStruct((B, S, nq), F32)],
        compiler_params=_cparams(("parallel", "parallel")),
        name="post_mix",
    )(x, y_m, y_d, proj, proj, gate1, scale2, shift2, g_post, g_pre, w_br_m, w_br_d, w_out, w_query)


SUBLANES = 8


def _sort16_pairs():
    n, out, p = P_TOPK, [], 1
    while p < n:
        k = p
        while k >= 1:
            for j in range(k % p, n - k, 2 * k):
                for i in range(min(k, n - j - k)):
                    if (i + j) // (2 * p) == (i + j + k) // (2 * p):
                        out.append((i + j, i + j + k))
            k //= 2
        p *= 2
    return out


def _exchange(vs, i, j):
    hi, lo = jnp.maximum(vs[i], vs[j]), jnp.minimum(vs[i], vs[j])
    vs[i], vs[j] = hi, lo


def _merge_over_sublanes(vs):
    for shift in (4, 2, 1):
        other = [pltpu.roll(v, shift, axis=0) for v in vs]
        vs = [jnp.maximum(vs[k], other[P_TOPK - 1 - k]) for k in range(P_TOPK)]
        d = P_TOPK // 2
        while d >= 1:
            for k in range(P_TOPK):
                if k & d == 0:
                    _exchange(vs, k, k + d)
            d //= 2
    return vs


def _top16_of_keys(slices):
    vs = list(slices)
    for i, j in _sort16_pairs():
        _exchange(vs, i, j)
    return _merge_over_sublanes(vs)


def _route_kernel(q_ref, keys_ref, bt_ref, beta_ref, ea_ref, eb_ref, *, tb_out):
    n_groups = N_KEYS // SUBLANES
    raw = []
    for c in range(2):
        q = q_ref[:, c * P_HALF:(c + 1) * P_HALF].astype(keys_ref.dtype)
        sc = lax.dot_general(keys_ref[c], q, NT_DIMS, preferred_element_type=F32)
        raw.append([sc[g * SUBLANES:(g + 1) * SUBLANES, :] for g in range(n_groups)])
    a_rows, b_rows = raw
    a_top = _top16_of_keys(a_rows)
    b_top = _top16_of_keys(b_rows)

    sub = lax.broadcasted_iota(jnp.int32, a_top[0].shape, 0)
    b_lo, b_hi = b_top[0], b_top[SUBLANES]
    for qq in range(1, SUBLANES):
        b_lo = jnp.where(sub == qq, b_top[qq], b_lo)
        b_hi = jnp.where(sub == qq, b_top[SUBLANES + qq], b_hi)
    cand = []
    for p in range(P_TOPK):
        n_valid = P_TOPK // (p + 1)
        cp = a_top[p] + b_lo
        cand.append(cp if n_valid >= SUBLANES else jnp.where(sub < n_valid, cp, -jnp.inf))
    extra = a_top[0] + b_hi
    for k in range(P_TOPK):
        cand[k], extra = jnp.maximum(cand[k], extra), jnp.minimum(cand[k], extra)
    best = _merge_over_sublanes(cand)

    tau = best[P_TOPK - 1]
    z = jnp.ones_like(tau)
    for k in range(1, P_TOPK):
        z = z + jnp.exp(best[k] - best[0])
    inv_z = 1.0 / z
    for g in range(n_groups):
        rows = pl.ds(g * SUBLANES, SUBLANES)
        beta = jnp.full_like(tau, jnp.inf)
        for qq in range(P_TOPK):
            beta = jnp.where(a_rows[g] + b_top[qq] >= tau, b_top[qq], beta)
        outs = ((beta_ref, beta), (bt_ref, b_rows[g]), (ea_ref, jnp.exp(a_rows[g] - a_top[0])),
                (eb_ref, jnp.exp(b_rows[g] - b_top[0]) * inv_z))
        for ref, val in outs:
            for s in range(ref.shape[0]):
                ref[s, rows, :] = val[:, s * tb_out:(s + 1) * tb_out]


def _route(qry, keys, tb, tb_out):
    T = qry.shape[0]
    out_spec = pl.BlockSpec((tb // tb_out, None, N_KEYS, tb_out), lambda i, h: (i, h, 0, 0))
    out_shape = jax.ShapeDtypeStruct((T // tb_out, P_HEADS, N_KEYS, tb_out), F32)
    return pl.pallas_call(
        functools.partial(_route_kernel, tb_out=tb_out),
        grid=(T // tb, P_HEADS),
        in_specs=[pl.BlockSpec((tb, 2 * P_HALF), lambda i, h: (i, h)),
                  pl.BlockSpec((None, 2, N_KEYS, P_HALF), lambda i, h: (h, 0, 0, 0))],
        out_specs=[out_spec] * 4,
        out_shape=[out_shape] * 4,
        compiler_params=_cparams(("parallel", "parallel")),
        name="peer_route",
    )(qry, keys)


def _peer_kernel(ht_ref, u_ref, vt_ref, bt_ref, beta_ref, ea_ref, eb_ref, x1_ref, gate2_ref, gpost_ref, o_ref,
                 act_a, act_b, acc_scr, skew_scr, *, n_groups, rows_per_group):
    j = pl.program_id(1)
    group = rows_per_group * N_KEYS
    tb = acc_scr.shape[1]

    @pl.when(j == 0)
    def _():
        acc_scr[...] = jnp.zeros_like(acc_scr)
        for h in range(P_HEADS):
            skew_scr[2 * h, 0:N_KEYS, 0:tb] = bt_ref[h]
            skew_scr[2 * h + 1, 0:N_KEYS, 0:tb] = eb_ref[h]

    def pre_activations(p):
        rows = pl.ds(pl.multiple_of(p * group, group), group)
        return jnp.dot(u_ref[rows, :], ht_ref[...], preferred_element_type=F32)

    def gated(act_ref, p):
        parts = []
        for g in range(rows_per_group):
            key_row = (j * n_groups + p) * rows_per_group + g
            gate = None
            for h in range(P_HEADS):
                picked = skew_scr[2 * h, 0:N_KEYS, 0:tb] >= beta_ref[h, pl.ds(key_row, 1), :]
                term = ea_ref[h, pl.ds(key_row, 1), :] * jnp.where(picked, skew_scr[2 * h + 1, 0:N_KEYS, 0:tb], 0.0)
                gate = term if gate is None else gate + term
            a = act_ref[g * N_KEYS:(g + 1) * N_KEYS, :]
            gelu = a * (0.5 + 0.5 * lax.erf(a * (2.0 ** -0.5)))
            parts.append((gelu * gate).astype(BF16))
        return jnp.concatenate(parts, axis=0)

    def accumulate(p, w):
        cols = pl.ds(pl.multiple_of(p * group, group), group)
        acc_scr[...] += jnp.dot(vt_ref[:, cols], w, preferred_element_type=F32)

    act_a[...] = pre_activations(0)

    def two_groups(m, carry):
        p0 = 2 * m
        act_b[...] = pre_activations(p0 + 1)
        accumulate(p0, gated(act_a, p0))
        act_a[...] = pre_activations(jnp.minimum(p0 + 2, n_groups - 1))
        accumulate(p0 + 1, gated(act_b, p0 + 1))
        return carry

    lax.fori_loop(0, n_groups // 2, two_groups, 0)

    @pl.when(j == pl.num_programs(1) - 1)
    def _():
        o_ref[...] = x1_ref[...] + gate2_ref[...] * _rms(acc_scr[...].T, gpost_ref[...])


def _peer(h2, u, v, b_t, beta_t, ea_t, eb_t, x1, gate2, g_post, seq_len, tb, eb, rows_per_group):
    T, D = h2.shape
    E = u.shape[0]
    blocks_per_seq = seq_len // tb
    group = rows_per_group * N_KEYS
    h2_t = jnp.transpose(h2.reshape(T // tb, tb, D), (0, 2, 1))
    v_t = jnp.transpose(v.reshape(E // eb, eb, D), (0, 2, 1))
    route_spec = pl.BlockSpec((None, P_HEADS, N_KEYS, tb), lambda i, j: (i, 0, 0, 0))
    return pl.pallas_call(
        functools.partial(_peer_kernel, n_groups=eb // group, rows_per_group=rows_per_group),
        grid=(T // tb, E // eb),
        in_specs=[pl.BlockSpec((None, D, tb), lambda i, j: (i, 0, 0)),
                  pl.BlockSpec((eb, D), lambda i, j: (j, 0)),
                  pl.BlockSpec((None, D, eb), lambda i, j: (j, 0, 0)),
                  route_spec, route_spec, route_spec, route_spec,
                  pl.BlockSpec((tb, D), lambda i, j: (i, 0)),
                  pl.BlockSpec((None, 1, D), lambda i, j: (i // blocks_per_seq, 0, 0)),
                  pl.BlockSpec((1, D), lambda i, j: (0, 0))],
        out_specs=pl.BlockSpec((tb, D), lambda i, j: (i, 0)),
        out_shape=jax.ShapeDtypeStruct((T, D), F32),
        scratch_shapes=[pltpu.VMEM((group, tb), F32), pltpu.VMEM((group, tb), F32), pltpu.VMEM((D, tb), F32),
                        pltpu.VMEM((2 * P_HEADS, N_KEYS + SUBLANES, tb + 128), F32)],
        compiler_params=_cparams(("parallel", "arbitrary")),
        name="peer_dense",
    )(h2_t, u, v_t, b_t, beta_t, ea_t, eb_t, x1, gate2, g_post)


def _pick(n, prefs):
    for p in prefs:
        if n % p == 0:
            return p
    raise ValueError(f"no supported tile for extent {n}")


def kernel(x, c, w_ada, b_ada, g_pre_mix, g_post_mix, g_pre_ffn, g_post_ffn, w_in, b_if, conv_w, conv_b,
           m_head_gain, lam_q1, lam_k1, lam_q2, lam_k2, d_head_gain, w_br_m, w_br_d, w_out,
           w_query, sub_keys, expert_u, expert_v):
    B, S, D = x.shape
    depth = w_ada.shape[0]
    assert depth == 1 and B <= 8
    m_width = M_HEADS * M_HEAD_DIM
    gate_off = 4 * m_width
    n_gate = 2 * M_HEADS

    for l in range(depth):
        c8 = jnp.pad(c, ((0, 8 - B), (0, 0)))
        ada = _ada(c8, w_ada[l], b_ada[l][None, :])[:B]
        shift1, scale1, gate1, shift2, scale2, gate2 = [a[:, None, :] for a in jnp.split(ada, ADA_PARTS, axis=-1)]

        w_in_l = w_in[l]
        w_main = jnp.concatenate([w_in_l[:, :gate_off], w_in_l[:, gate_off + n_gate:]], axis=1).astype(BF16)
        w_if = jnp.pad(w_in_l[:, gate_off:gate_off + n_gate], ((0, 0), (0, 128 - n_gate))).astype(BF16)
        b_if_p = jnp.pad(b_if[l], (0, 128 - n_gate))[None, :]

        proj, gates = _inproj(x, g_pre_mix[l][None, :], scale1, shift1, w_main, w_if, b_if_p,
                              tm=_pick(S, (1024, 512, 256)), tn=_pick(w_main.shape[1], (2304, 1152, 1024)))
        gates_rows = jnp.transpose(gates[:, :, :n_gate], (0, 2, 1))[:, :, None, :]

        y_m = _mlstm(proj, gates_rows, conv_w[l], conv_b[l][None, :], m_head_gain[l][None, :],
                     L=_pick(S, (256,)))
        y_d = _attn(proj, lam_q1[l][None, :], lam_k1[l][None, :], lam_q2[l][None, :], lam_k2[l][None, :],
                    d_head_gain[l][None, :], tq=_pick(S, (512, 256)), tk=_pick(S, (512, 256)))

        x1, h2, qry = _post(x, y_m, y_d, proj, gate1, scale2, shift2, g_post_mix[l][None, :],
                            g_pre_ffn[l][None, :], w_br_m[l].astype(BF16), w_br_d[l].astype(BF16),
                            w_out[l].astype(BF16), w_query[l].astype(BF16), tm=_pick(S, (512, 256)))

        T = B * S
        tb_peer = _pick(S, (512, 256))
        b_t, beta_t, ea_t, eb_t = _route(qry.reshape(T, -1), sub_keys[l].astype(BF16),
                                         tb=_pick(T, (2 * tb_peer,)), tb_out=tb_peer)
        x = _peer(h2.reshape(T, D), expert_u[l].astype(BF16), expert_v[l].astype(BF16), b_t, beta_t, ea_t, eb_t,
                  x1.reshape(T, D), gate2, g_post_ffn[l][None, :], seq_len=S,
                  tb=tb_peer, eb=2048, rows_per_group=4).reshape(B, S, D)
    return x
```

```python
import functools
import math

import jax
import jax.numpy as jnp
from jax import lax
from jax.experimental import pallas as pl
from jax.experimental.pallas import tpu as pltpu

F32 = jnp.float32
BF16 = jnp.bfloat16
HIGHEST = lax.Precision.HIGHEST

EPS = 1e-6
ADA_PARTS = 6
CHUNK = 64
M_HEADS = 4
M_HEAD_DIM = 256
CONV_WIDTH = 4
D_HEADS = 4
D_QK_DIM = 128
D_V_DIM = 256
P_HEADS = 8
N_KEYS = 128
P_TOPK = 16
P_HALF = 128
LAM_INIT = 0.8 - 0.6 * math.exp(-0.3 * 0)

V7X_VMEM_LIMIT_BYTES = 56 * 1024 * 1024

NT_DIMS = (((1,), (1,)), ((), ()))
TN_DIMS = (((0,), (0,)), ((), ()))


def _cparams(semantics):
    return pltpu.CompilerParams(dimension_semantics=semantics, vmem_limit_bytes=V7X_VMEM_LIMIT_BYTES)


def _rms(x, gain):
    return x * lax.rsqrt(jnp.mean(x * x, axis=-1, keepdims=True) + EPS) * gain


def _ada_kernel(c_ref, w_ref, b_ref, o_ref):
    c = c_ref[...]
    sc = c * jax.nn.sigmoid(c)
    o_ref[...] = jnp.dot(sc.astype(BF16), w_ref[...].astype(BF16), preferred_element_type=F32) + b_ref[...]


def _ada(c8, w_ada, b_ada):
    d = c8.shape[1]
    n = w_ada.shape[1]
    tn = 1024
    return pl.pallas_call(
        _ada_kernel,
        grid=(n // tn,),
        in_specs=[pl.BlockSpec((8, d), lambda j: (0, 0)),
                  pl.BlockSpec((d, tn), lambda j: (0, j)),
                  pl.BlockSpec((1, tn), lambda j: (0, j))],
        out_specs=pl.BlockSpec((8, tn), lambda j: (0, j)),
        out_shape=jax.ShapeDtypeStruct((8, n), F32),
        compiler_params=_cparams(("arbitrary",)),
        name="ada",
    )(c8, w_ada, b_ada)


def _inproj_kernel(x_ref, g_ref, sc_ref, sh_ref, w_ref, wif_ref, bif_ref, o_ref, oif_ref, h_scr):
    @pl.when(pl.program_id(2) == 0)
    def _():
        h = _rms(x_ref[...], g_ref[...]) * (1.0 + sc_ref[...]) + sh_ref[...]
        hb = h.astype(BF16)
        h_scr[...] = hb
        oif_ref[...] = jnp.dot(hb, wif_ref[...], preferred_element_type=F32) + bif_ref[...]

    o_ref[...] = jnp.dot(h_scr[...], w_ref[...], preferred_element_type=F32).astype(o_ref.dtype)


def _inproj(x, gain, scale, shift, w_main, w_if, b_if, tm, tn):
    B, S, D = x.shape
    n = w_main.shape[1]
    return pl.pallas_call(
        _inproj_kernel,
        grid=(B, S // tm, n // tn),
        in_specs=[pl.BlockSpec((None, tm, D), lambda b, i, j: (b, i, 0)),
                  pl.BlockSpec((1, D), lambda b, i, j: (0, 0)),
                  pl.BlockSpec((None, 1, D), lambda b, i, j: (b, 0, 0)),
                  pl.BlockSpec((None, 1, D), lambda b, i, j: (b, 0, 0)),
                  pl.BlockSpec((D, tn), lambda b, i, j: (0, j)),
                  pl.BlockSpec((D, 128), lambda b, i, j: (0, 0)),
                  pl.BlockSpec((1, 128), lambda b, i, j: (0, 0))],
        out_specs=[pl.BlockSpec((None, tm, tn), lambda b, i, j: (b, i, j)),
                   pl.BlockSpec((None, tm, 128), lambda b, i, j: (b, i, 0))],
        out_shape=[jax.ShapeDtypeStruct((B, S, n), BF16),
                   jax.ShapeDtypeStruct((B, S, 128), F32)],
        scratch_shapes=[pltpu.VMEM((tm, D), BF16)],
        compiler_params=_cparams(("parallel", "parallel", "arbitrary")),
        name="inproj",
    )(x, gain, scale, shift, w_main, w_if, b_if)


def _mlstm_kernel(q_ref, k_ref, v_ref, mo_ref, cwq_ref, cwk_ref, cbq_ref, cbk_ref, gi_ref, gf_ref, gain_ref,
                  y_ref, ct_scr, n_scr, m_scr, qtail, ktail, xext, *, L):
    @pl.when(pl.program_id(2) == 0)
    def _():
        ct_scr[...] = jnp.zeros_like(ct_scr)
        n_scr[...] = jnp.zeros_like(n_scr)
        m_scr[...] = jnp.zeros_like(m_scr)
        qtail[...] = jnp.zeros_like(qtail)
        ktail[...] = jnp.zeros_like(ktail)

    def conv_silu(x_ref, tail, cw_ref, cb_ref):
        x = x_ref[...].astype(F32)
        xext[0:8, :] = tail[...]
        xext[8:, :] = x
        acc = jnp.broadcast_to(cb_ref[...], x.shape)
        for j in range(CONV_WIDTH):
            acc = acc + cw_ref[j:j + 1, :] * xext[5 + j:5 + j + L, :]
        tail[...] = x[L - 8:, :]
        return acc * jax.nn.sigmoid(acc)

    q = conv_silu(q_ref, qtail, cwq_ref, cbq_ref)
    k = conv_silu(k_ref, ktail, cwk_ref, cbk_ref) * (M_HEAD_DIM ** -0.5)
    v = v_ref[...]
    qb = q.astype(BF16)
    kb = k.astype(BF16)

    li_row = gi_ref[...]
    fp = gf_ref[...]
    lf_row = jnp.minimum(fp, 0.0) - jnp.log1p(jnp.exp(-jnp.abs(fp)))

    ti = lax.broadcasted_iota(jnp.int32, (L, L), 0)
    si = lax.broadcasted_iota(jnp.int32, (L, L), 1)
    causal = si <= ti
    tril = causal.astype(BF16)
    eye = (si == ti).astype(BF16)

    def bf16_pieces(x):
        hi = x.astype(BF16).astype(F32)
        mid = (x - hi).astype(BF16).astype(F32)
        lo = (x - hi - mid).astype(BF16).astype(F32)
        return hi, mid, lo

    pieces = bf16_pieces(lf_row) + bf16_pieces(li_row)
    rid = lax.broadcasted_iota(jnp.int32, (8, L), 0)
    rows = jnp.zeros((8, L), F32)
    for idx, piece in enumerate(pieces):
        rows = jnp.where(rid == idx, piece, rows)
    rows = rows.astype(BF16)
    cum_rows = lax.dot_general(rows, tril, NT_DIMS, preferred_element_type=F32)
    cum_cols = lax.dot_general(tril, rows, NT_DIMS, preferred_element_type=F32)
    raw_cols = lax.dot_general(eye, rows, NT_DIMS, preferred_element_type=F32)
    b_row = cum_rows[0:1, :] + cum_rows[1:2, :] + cum_rows[2:3, :]
    b_col = cum_cols[:, 0:1] + cum_cols[:, 1:2] + cum_cols[:, 2:3]
    li_col = raw_cols[:, 3:4] + raw_cols[:, 4:5] + raw_cols[:, 5:6]

    m_prev = m_scr[0:1, 0:1]
    a_col = b_col + m_prev
    dmat = jnp.where(causal, b_col - b_row + li_row, -jnp.inf)
    m_t = jnp.maximum(a_col, jnp.max(dmat, axis=1, keepdims=True))
    p = jnp.exp(dmat - m_t)
    s = lax.dot_general(qb, kb, NT_DIMS, preferred_element_type=F32) * p
    w_inter = jnp.exp(a_col - m_t)
    ct = ct_scr[...]
    num = (jnp.dot(s.astype(BF16), v, preferred_element_type=F32)
           + w_inter * jnp.dot(qb, ct.astype(BF16), preferred_element_type=F32))
    n_row = n_scr[...]
    den = jnp.sum(s, axis=1, keepdims=True) + w_inter * jnp.sum(q * n_row, axis=1, keepdims=True)
    h = num / jnp.maximum(jnp.abs(den), jnp.exp(-m_t))

    m_new = m_t[L - 1:L, :]
    b_last = b_col[L - 1:L, :]
    g_prev = jnp.exp(b_last + m_prev - m_new)
    gs_col = jnp.exp(b_last - b_col + li_col - m_new)
    gv = (gs_col * v.astype(F32)).astype(BF16)
    ct_scr[...] = g_prev * ct + lax.dot_general(kb, gv, TN_DIMS, preferred_element_type=F32)
    n_scr[...] = g_prev * n_row + jnp.sum(gs_col * k, axis=0, keepdims=True)
    m_scr[...] = jnp.broadcast_to(m_new, m_scr.shape)

    y = _rms(h, gain_ref[...]) * jax.nn.sigmoid(mo_ref[...].astype(F32))
    y_ref[...] = y.astype(y_ref.dtype)


def _mlstm(proj, gates_rows, conv_w, conv_b, head_gain, L):
    B, S, _ = proj.shape
    H, Dh = M_HEADS, M_HEAD_DIM
    col = lambda off: (lambda b, h, c: (b, c, off + h))
    return pl.pallas_call(
        functools.partial(_mlstm_kernel, L=L),
        grid=(B, H, S // L),
        in_specs=[pl.BlockSpec((None, L, Dh), col(0)),
                  pl.BlockSpec((None, L, Dh), col(H)),
                  pl.BlockSpec((None, L, Dh), col(2 * H)),
                  pl.BlockSpec((None, L, Dh), col(3 * H)),
                  pl.BlockSpec((CONV_WIDTH, Dh), lambda b, h, c: (0, h)),
                  pl.BlockSpec((CONV_WIDTH, Dh), lambda b, h, c: (0, H + h)),
                  pl.BlockSpec((1, Dh), lambda b, h, c: (0, h)),
                  pl.BlockSpec((1, Dh), lambda b, h, c: (0, H + h)),
                  pl.BlockSpec((None, None, 1, L), lambda b, h, c: (b, h, 0, c)),
                  pl.BlockSpec((None, None, 1, L), lambda b, h, c: (b, H + h, 0, c)),
                  pl.BlockSpec((1, Dh), lambda b, h, c: (0, h))],
        out_specs=pl.BlockSpec((None, L, Dh), lambda b, h, c: (b, c, h)),
        out_shape=jax.ShapeDtypeStruct((B, S, H * Dh), BF16),
        scratch_shapes=[pltpu.VMEM((Dh, Dh), F32), pltpu.VMEM((1, Dh), F32), pltpu.VMEM((8, 128), F32),
                        pltpu.VMEM((8, Dh), F32), pltpu.VMEM((8, Dh), F32), pltpu.VMEM((L + 8, Dh), F32)],
        compiler_params=_cparams(("parallel", "parallel", "arbitrary")),
        name="mlstm",
    )(proj, proj, proj, proj, conv_w, conv_w, conv_b, conv_b, gates_rows, gates_rows, head_gain)


def _attn_kernel(q_ref, k_ref, v_ref, lq1_ref, lk1_ref, lq2_ref, lk2_ref, gain_ref, o_ref,
                 m_scr, l_scr, acc_scr, s_a, s_b, *, tq, tk):
    assert tq == tk
    qi = pl.program_id(2)
    c2 = (D_QK_DIM ** -0.5) * math.log2(math.e)

    m_scr[...] = jnp.full_like(m_scr, -jnp.inf)
    l_scr[...] = jnp.zeros_like(l_scr)
    acc_scr[...] = jnp.zeros_like(acc_scr)

    def scores(dst, k0, diagonal):
        for c in range(2):
            q = q_ref[:, c * D_QK_DIM:(c + 1) * D_QK_DIM]
            k = k_ref[pl.ds(k0, tk), c * D_QK_DIM:(c + 1) * D_QK_DIM]
            s = lax.dot_general(q, k, NT_DIMS, preferred_element_type=F32)
            if diagonal:
                tch = lax.broadcasted_iota(jnp.int32, s.shape, 0) // CHUNK
                sch = lax.broadcasted_iota(jnp.int32, s.shape, 1) // CHUNK
                s = jnp.where(sch <= tch, s, -jnp.inf)
            dst[c] = s

    def fold(src, k0):
        v = v_ref[pl.ds(k0, tk), :]
        for c in range(2):
            s = src[c]
            m_old = m_scr[c]
            m_new = jnp.maximum(m_old, jnp.max(s, axis=1, keepdims=True))
            alpha = jnp.exp2((m_old - m_new) * c2)
            p = jnp.exp2((s - jnp.concatenate([m_new] * (tk // 128), axis=1)) * c2)
            p_lanes = p[:, 0:128]
            for g in range(1, tk // 128):
                p_lanes = p_lanes + p[:, g * 128:(g + 1) * 128]
            l_scr[c] = alpha * l_scr[c] + p_lanes
            acc_scr[c] = (jnp.concatenate([alpha] * (D_V_DIM // 128), axis=1) * acc_scr[c]
                          + jnp.dot(p.astype(BF16), v, preferred_element_type=F32))
            m_scr[c] = m_new

    n_full = qi
    q0 = pl.multiple_of(qi * tq, tq)
    scores(s_a, 0, False)

    def two_chunks(m, carry):
        k0 = pl.multiple_of(2 * m * tk, tk)
        scores(s_b, k0 + tk, False)
        fold(s_a, k0)
        scores(s_a, k0 + 2 * tk, False)
        fold(s_b, k0 + tk)
        return carry

    lax.fori_loop(0, n_full // 2, two_chunks, 0)

    @pl.when(n_full % 2 == 1)
    def _():
        scores(s_b, q0, True)
        fold(s_a, q0 - tk)
        fold(s_b, q0)

    @pl.when(n_full % 2 == 0)
    def _():
        scores(s_b, q0, True)
        fold(s_b, q0)

    lam = (jnp.exp(jnp.sum(lq1_ref[...] * lk1_ref[...], axis=1, keepdims=True))
           - jnp.exp(jnp.sum(lq2_ref[...] * lk2_ref[...], axis=1, keepdims=True)) + LAM_INIT)
    l0 = jnp.sum(l_scr[0], axis=1, keepdims=True)
    l1 = jnp.sum(l_scr[1], axis=1, keepdims=True)
    o = acc_scr[0] / l0 - lam * (acc_scr[1] / l1)
    o_ref[...] = (_rms(o, gain_ref[...]) * (1.0 - LAM_INIT)).astype(o_ref.dtype)


def _attn(proj, lam_q1, lam_k1, lam_q2, lam_k2, head_gain, tq, tk):
    B, S, _ = proj.shape
    H = D_HEADS
    blk0 = 4 * M_HEADS
    lam_spec = pl.BlockSpec((1, D_QK_DIM), lambda b, h, qi: (0, 0))
    return pl.pallas_call(
        functools.partial(_attn_kernel, tq=tq, tk=tk),
        grid=(B, H, S // tq),
        in_specs=[pl.BlockSpec((None, tq, 256), lambda b, h, qi: (b, qi, blk0 + h)),
                  pl.BlockSpec((None, S, 256), lambda b, h, qi: (b, 0, blk0 + H + h)),
                  pl.BlockSpec((None, S, 256), lambda b, h, qi: (b, 0, blk0 + 2 * H + h)),
                  lam_spec, lam_spec, lam_spec, lam_spec,
                  pl.BlockSpec((1, D_V_DIM), lambda b, h, qi: (0, h))],
        out_specs=pl.BlockSpec((None, tq, D_V_DIM), lambda b, h, qi: (b, qi, h)),
        out_shape=jax.ShapeDtypeStruct((B, S, H * D_V_DIM), BF16),
        scratch_shapes=[pltpu.VMEM((2, tq, 128), F32), pltpu.VMEM((2, tq, 128), F32),
                        pltpu.VMEM((2, tq, D_V_DIM), F32),
                        pltpu.VMEM((2, tq, tk), F32), pltpu.VMEM((2, tq, tk), F32)],
        compiler_params=_cparams(("parallel", "parallel", "arbitrary")),
        name="diff_attn",
    )(proj, proj, proj, lam_q1, lam_k1, lam_q2, lam_k2, head_gain)


def _post_kernel(x_ref, ym_ref, yd_ref, gm_ref, gd_ref, gate1_ref, sc2_ref, sh2_ref, gpost_ref, gpre_ref,
                 wbm_ref, wbd_ref, wo_ref, wq_ref, x1_ref, h2_ref, qry_ref):
    bm = jnp.dot(ym_ref[...], wbm_ref[...], preferred_element_type=F32)
    bd = jnp.dot(yd_ref[...], wbd_ref[...], preferred_element_type=F32)
    merged = jax.nn.sigmoid(gm_ref[...].astype(F32)) * bm + jax.nn.sigmoid(gd_ref[...].astype(F32)) * bd
    y = jnp.dot(merged.astype(BF16), wo_ref[...], preferred_element_type=F32)
    x1 = x_ref[...] + gate1_ref[...] * _rms(y, gpost_ref[...])
    x1_ref[...] = x1
    h2 = (_rms(x1, gpre_ref[...]) * (1.0 + sc2_ref[...]) + sh2_ref[...]).astype(BF16)
    h2_ref[...] = h2
    qry_ref[...] = jnp.dot(h2, wq_ref[...], preferred_element_type=F32)


def _post(x, y_m, y_d, proj, gate1, scale2, shift2, g_post, g_pre, w_br_m, w_br_d, w_out, w_query, tm):
    B, S, D = x.shape
    nq = w_query.shape[1]
    row = pl.BlockSpec((None, tm, D), lambda b, i: (b, i, 0))
    per_b = pl.BlockSpec((None, 1, D), lambda b, i: (b, 0, 0))
    vec = pl.BlockSpec((1, D), lambda b, i: (0, 0))
    wsq = pl.BlockSpec((D, D), lambda b, i: (0, 0))
    return pl.pallas_call(
        _post_kernel,
        grid=(B, S // tm),
        in_specs=[row, row, row,
                  pl.BlockSpec((None, tm, D), lambda b, i: (b, i, 7)),
                  pl.BlockSpec((None, tm, D), lambda b, i: (b, i, 8)),
                  per_b, per_b, per_b, vec, vec, wsq, wsq, wsq,
                  pl.BlockSpec((D, nq), lambda b, i: (0, 0))],
        out_specs=[row, row, pl.BlockSpec((None, tm, nq), lambda b, i: (b, i, 0))],
        out_shape=[jax.ShapeDtypeStruct((B, S, D), F32),
                   jax.ShapeDtypeStruct((B, S, D), BF16),
                   jax.ShapeDtypeStruct((B, S, nq), F32)],
        compiler_params=_cparams(("parallel", "parallel")),
        name="post_mix",
    )(x, y_m, y_d, proj, proj, gate1, scale2, shift2, g_post, g_pre, w_br_m, w_br_d, w_out, w_query)


SUBLANES = 8


def _sort16_pairs():
    n, out, p = P_TOPK, [], 1
    while p < n:
        k = p
        while k >= 1:
            for j in range(k % p, n - k, 2 * k):
                for i in range(min(k, n - j - k)):
                    if (i + j) // (2 * p) == (i + j + k) // (2 * p):
                        out.append((i + j, i + j + k))
            k //= 2
        p *= 2
    return out


def _exchange(vs, i, j):
    hi, lo = jnp.maximum(vs[i], vs[j]), jnp.minimum(vs[i], vs[j])
    vs[i], vs[j] = hi, lo


def _merge_over_sublanes(vs):
    for shift in (4, 2, 1):
        other = [pltpu.roll(v, shift, axis=0) for v in vs]
        vs = [jnp.maximum(vs[k], other[P_TOPK - 1 - k]) for k in range(P_TOPK)]
        d = P_TOPK // 2
        while d >= 1:
            for k in range(P_TOPK):
                if k & d == 0:
                    _exchange(vs, k, k + d)
            d //= 2
    return vs


def _top16_of_keys(slices):
    vs = list(slices)
    for i, j in _sort16_pairs():
        _exchange(vs, i, j)
    return _merge_over_sublanes(vs)


def _route_kernel(q_ref, keys_ref, rank_ref, cnt_ref, ea_ref, eb_ref, *, tb_out):
    n_groups = N_KEYS // SUBLANES
    raw = []
    for c in range(2):
        q = q_ref[:, c * P_HALF:(c + 1) * P_HALF].astype(keys_ref.dtype)
        sc = lax.dot_general(keys_ref[c], q, NT_DIMS, preferred_element_type=F32)
        raw.append([sc[g * SUBLANES:(g + 1) * SUBLANES, :] for g in range(n_groups)])
    a_rows, b_rows = raw
    a_top = _top16_of_keys(a_rows)
    b_top = _top16_of_keys(b_rows)

    sub = lax.broadcasted_iota(jnp.int32, a_top[0].shape, 0)
    b_lo, b_hi = b_top[0], b_top[SUBLANES]
    for qq in range(1, SUBLANES):
        b_lo = jnp.where(sub == qq, b_top[qq], b_lo)
        b_hi = jnp.where(sub == qq, b_top[SUBLANES + qq], b_hi)
    cand = []
    for p in range(P_TOPK):
        n_valid = P_TOPK // (p + 1)
        cp = a_top[p] + b_lo
        cand.append(cp if n_valid >= SUBLANES else jnp.where(sub < n_valid, cp, -jnp.inf))
    extra = a_top[0] + b_hi
    for k in range(P_TOPK):
        cand[k], extra = jnp.maximum(cand[k], extra), jnp.minimum(cand[k], extra)
    best = _merge_over_sublanes(cand)

    tau = best[P_TOPK - 1]
    z = jnp.ones_like(tau)
    for k in range(1, P_TOPK):
        z = z + jnp.exp(best[k] - best[0])
    half_inv_z = 0.5 / z

    def store(ref, row0, val):
        for s in range(ref.shape[0]):
            ref[s, pl.ds(row0, val.shape[0]), :] = val[:, s * tb_out:(s + 1) * tb_out].astype(ref.dtype)

    ranks, weights = [], []
    for g in range(n_groups):
        count = jnp.zeros_like(tau)
        for qq in range(P_TOPK):
            count = jnp.where(a_rows[g] + b_top[qq] >= tau, float(qq + 1), count)
        rank = jnp.full_like(tau, float(P_TOPK))
        for qq in reversed(range(P_TOPK)):
            rank = jnp.where(b_top[qq] <= b_rows[g], float(qq), rank)
        store(cnt_ref, g * SUBLANES, count)
        store(ea_ref, g * SUBLANES, jnp.exp(a_rows[g] - a_top[0]))
        ranks.append(rank)
        weights.append(jnp.exp(b_rows[g] - b_top[0]) * half_inv_z)
        if g % 2 == 1:
            store(rank_ref, (g - 1) * SUBLANES, jnp.concatenate(ranks[-2:], axis=0))
            store(eb_ref, (g - 1) * SUBLANES, jnp.concatenate(weights[-2:], axis=0))


def _route(qry, keys, tb, tb_out):
    T = qry.shape[0]
    out_spec = pl.BlockSpec((tb // tb_out, None, N_KEYS, tb_out), lambda i, h: (i, h, 0, 0))
    shape = (T // tb_out, P_HEADS, N_KEYS, tb_out)
    return pl.pallas_call(
        functools.partial(_route_kernel, tb_out=tb_out),
        grid=(T // tb, P_HEADS),
        in_specs=[pl.BlockSpec((tb, 2 * P_HALF), lambda i, h: (i, h)),
                  pl.BlockSpec((None, 2, N_KEYS, P_HALF), lambda i, h: (h, 0, 0, 0))],
        out_specs=[out_spec] * 4,
        out_shape=[jax.ShapeDtypeStruct(shape, BF16), jax.ShapeDtypeStruct(shape, F32),
                   jax.ShapeDtypeStruct(shape, F32), jax.ShapeDtypeStruct(shape, BF16)],
        compiler_params=_cparams(("parallel", "parallel")),
        name="peer_route",
    )(qry, keys)


def _peer_kernel(ht_ref, u_ref, vt_ref, rank_ref, cnt_ref, ea_ref, eb_ref, x1_ref, gate2_ref, gpost_ref, o_ref,
                 act_a, act_b, acc_scr, skew_scr, *, n_groups, rows_per_group):
    j = pl.program_id(1)
    group = rows_per_group * N_KEYS
    tb = acc_scr.shape[1]

    @pl.when(j == 0)
    def _():
        acc_scr[...] = jnp.zeros_like(acc_scr)
        for h in range(P_HEADS):
            skew_scr[2 * h, 0:N_KEYS, 0:tb] = rank_ref[h]
            skew_scr[2 * h + 1, 0:N_KEYS, 0:tb] = eb_ref[h]

    def pre_activations(p):
        rows = pl.ds(pl.multiple_of(p * group, group), group)
        return jnp.dot(u_ref[rows, :], ht_ref[...], preferred_element_type=F32)

    def gated(act_ref, p):
        parts = []
        for g in range(rows_per_group):
            key_row = (j * n_groups + p) * rows_per_group + g
            gate = None
            for h in range(P_HEADS):
                count_row = cnt_ref[h, pl.ds(key_row, 1), :].astype(BF16)
                ea_row = ea_ref[h, pl.ds(key_row, 1), :].astype(BF16)
                picked = skew_scr[2 * h, 0:N_KEYS, 0:tb] < count_row
                term = ea_row * jnp.where(picked, skew_scr[2 * h + 1, 0:N_KEYS, 0:tb], jnp.zeros((), BF16))
                gate = term if gate is None else gate + term
            a = act_ref[g * N_KEYS:(g + 1) * N_KEYS, :]
            gelu2 = a * (1.0 + lax.erf(a * (2.0 ** -0.5)))
            parts.append(gelu2.astype(BF16) * gate)
        return jnp.concatenate(parts, axis=0)

    def accumulate(p, w):
        cols = pl.ds(pl.multiple_of(p * group, group), group)
        acc_scr[...] += jnp.dot(vt_ref[:, cols], w, preferred_element_type=F32)

    act_a[...] = pre_activations(0)

    def two_groups(m, carry):
        p0 = 2 * m
        act_b[...] = pre_activations(p0 + 1)
        accumulate(p0, gated(act_a, p0))
        act_a[...] = pre_activations(jnp.minimum(p0 + 2, n_groups - 1))
        accumulate(p0 + 1, gated(act_b, p0 + 1))
        return carry

    lax.fori_loop(0, n_groups // 2, two_groups, 0)

    @pl.when(j == pl.num_programs(1) - 1)
    def _():
        o_ref[...] = x1_ref[...] + gate2_ref[...] * _rms(acc_scr[...].T, gpost_ref[...])


def _peer(h2, u, v, rank_t, cnt_t, ea_t, eb_t, x1, gate2, g_post, seq_len, tb, eb, rows_per_group):
    T, D = h2.shape
    E = u.shape[0]
    blocks_per_seq = seq_len // tb
    group = rows_per_group * N_KEYS
    h2_t = jnp.transpose(h2.reshape(T // tb, tb, D), (0, 2, 1))
    v_t = jnp.transpose(v.reshape(E // eb, eb, D), (0, 2, 1))
    route_spec = pl.BlockSpec((None, P_HEADS, N_KEYS, tb), lambda i, j: (i, 0, 0, 0))
    return pl.pallas_call(
        functools.partial(_peer_kernel, n_groups=eb // group, rows_per_group=rows_per_group),
        grid=(T // tb, E // eb),
        in_specs=[pl.BlockSpec((None, D, tb), lambda i, j: (i, 0, 0)),
                  pl.BlockSpec((eb, D), lambda i, j: (j, 0)),
                  pl.BlockSpec((None, D, eb), lambda i, j: (j, 0, 0)),
                  route_spec, route_spec, route_spec, route_spec,
                  pl.BlockSpec((tb, D), lambda i, j: (i, 0)),
                  pl.BlockSpec((None, 1, D), lambda i, j: (i // blocks_per_seq, 0, 0)),
                  pl.BlockSpec((1, D), lambda i, j: (0, 0))],
        out_specs=pl.BlockSpec((tb, D), lambda i, j: (i, 0)),
        out_shape=jax.ShapeDtypeStruct((T, D), F32),
        scratch_shapes=[pltpu.VMEM((group, tb), F32), pltpu.VMEM((group, tb), F32), pltpu.VMEM((D, tb), F32),
                        pltpu.VMEM((2 * P_HEADS, N_KEYS + 2 * SUBLANES, tb + 128), BF16)],
        compiler_params=_cparams(("parallel", "arbitrary")),
        name="peer_dense",
    )(h2_t, u, v_t, rank_t, cnt_t, ea_t, eb_t, x1, gate2, g_post)


def _pick(n, prefs):
    for p in prefs:
        if n % p == 0:
            return p
    raise ValueError(f"no supported tile for extent {n}")


def kernel(x, c, w_ada, b_ada, g_pre_mix, g_post_mix, g_pre_ffn, g_post_ffn, w_in, b_if, conv_w, conv_b,
           m_head_gain, lam_q1, lam_k1, lam_q2, lam_k2, d_head_gain, w_br_m, w_br_d, w_out,
           w_query, sub_keys, expert_u, expert_v):
    B, S, D = x.shape
    depth = w_ada.shape[0]
    assert depth == 1 and B <= 8
    m_width = M_HEADS * M_HEAD_DIM
    gate_off = 4 * m_width
    n_gate = 2 * M_HEADS

    for l in range(depth):
        c8 = jnp.pad(c, ((0, 8 - B), (0, 0)))
        ada = _ada(c8, w_ada[l], b_ada[l][None, :])[:B]
        shift1, scale1, gate1, shift2, scale2, gate2 = [a[:, None, :] for a in jnp.split(ada, ADA_PARTS, axis=-1)]

        w_in_l = w_in[l]
        w_main = jnp.concatenate([w_in_l[:, :gate_off], w_in_l[:, gate_off + n_gate:]], axis=1).astype(BF16)
        w_if = jnp.pad(w_in_l[:, gate_off:gate_off + n_gate], ((0, 0), (0, 128 - n_gate))).astype(BF16)
        b_if_p = jnp.pad(b_if[l], (0, 128 - n_gate))[None, :]

        proj, gates = _inproj(x, g_pre_mix[l][None, :], scale1, shift1, w_main, w_if, b_if_p,
                              tm=_pick(S, (1024, 512, 256)), tn=_pick(w_main.shape[1], (2304, 1152, 1024)))
        gates_rows = jnp.transpose(gates[:, :, :n_gate], (0, 2, 1))[:, :, None, :]

        y_m = _mlstm(proj, gates_rows, conv_w[l], conv_b[l][None, :], m_head_gain[l][None, :],
                     L=_pick(S, (256,)))
        y_d = _attn(proj, lam_q1[l][None, :], lam_k1[l][None, :], lam_q2[l][None, :], lam_k2[l][None, :],
                    d_head_gain[l][None, :], tq=_pick(S, (512, 256)), tk=_pick(S, (512, 256)))

        x1, h2, qry = _post(x, y_m, y_d, proj, gate1, scale2, shift2, g_post_mix[l][None, :],
                            g_pre_ffn[l][None, :], w_br_m[l].astype(BF16), w_br_d[l].astype(BF16),
                            w_out[l].astype(BF16), w_query[l].astype(BF16), tm=_pick(S, (512, 256)))

        T = B * S
        tb_peer = _pick(S, (512, 256))
        rank_t, cnt_t, ea_t, eb_t = _route(qry.reshape(T, -1), sub_keys[l].astype(BF16),
                                         tb=_pick(T, (2 * tb_peer,)), tb_out=tb_peer)
        x = _peer(h2.reshape(T, D), expert_u[l].astype(BF16), expert_v[l].astype(BF16), rank_t, cnt_t, ea_t, eb_t,
                  x1.reshape(T, D), gate2, g_post_ffn[l][None, :], seq_len=S,
                  tb=tb_peer, eb=2048, rows_per_group=4).reshape(B, S, D)
    return x
```

```python
import functools
import math

import jax
import jax.numpy as jnp
from jax import lax
from jax.experimental import pallas as pl
from jax.experimental.pallas import tpu as pltpu

F32 = jnp.float32
BF16 = jnp.bfloat16
HIGHEST = lax.Precision.HIGHEST

EPS = 1e-6
ADA_PARTS = 6
CHUNK = 64
M_HEADS = 4
M_HEAD_DIM = 256
CONV_WIDTH = 4
D_HEADS = 4
D_QK_DIM = 128
D_V_DIM = 256
P_HEADS = 8
N_KEYS = 128
P_TOPK = 16
P_HALF = 128
LAM_INIT = 0.8 - 0.6 * math.exp(-0.3 * 0)

V7X_VMEM_LIMIT_BYTES = 56 * 1024 * 1024

NT_DIMS = (((1,), (1,)), ((), ()))
TN_DIMS = (((0,), (0,)), ((), ()))


def _cparams(semantics):
    return pltpu.CompilerParams(dimension_semantics=semantics, vmem_limit_bytes=V7X_VMEM_LIMIT_BYTES)


def _rms(x, gain):
    return x * lax.rsqrt(jnp.mean(x * x, axis=-1, keepdims=True) + EPS) * gain


def _ada_kernel(c_ref, w_ref, b_ref, o_ref):
    c = c_ref[...]
    sc = c * jax.nn.sigmoid(c)
    o_ref[...] = jnp.dot(sc.astype(BF16), w_ref[...].astype(BF16), preferred_element_type=F32) + b_ref[...]


def _ada(c8, w_ada, b_ada):
    d = c8.shape[1]
    n = w_ada.shape[1]
    tn = 1024
    return pl.pallas_call(
        _ada_kernel,
        grid=(n // tn,),
        in_specs=[pl.BlockSpec((8, d), lambda j: (0, 0)),
                  pl.BlockSpec((d, tn), lambda j: (0, j)),
                  pl.BlockSpec((1, tn), lambda j: (0, j))],
        out_specs=pl.BlockSpec((8, tn), lambda j: (0, j)),
        out_shape=jax.ShapeDtypeStruct((8, n), F32),
        compiler_params=_cparams(("arbitrary",)),
        name="ada",
    )(c8, w_ada, b_ada)


def _inproj_kernel(x_ref, g_ref, sc_ref, sh_ref, w_ref, wif_ref, bif_ref, o_ref, oif_ref, h_scr):
    @pl.when(pl.program_id(2) == 0)
    def _():
        h = _rms(x_ref[...], g_ref[...]) * (1.0 + sc_ref[...]) + sh_ref[...]
        hb = h.astype(BF16)
        h_scr[...] = hb
        oif_ref[...] = jnp.dot(hb, wif_ref[...], preferred_element_type=F32) + bif_ref[...]

    o_ref[...] = jnp.dot(h_scr[...], w_ref[...], preferred_element_type=F32).astype(o_ref.dtype)


def _inproj(x, gain, scale, shift, w_main, w_if, b_if, tm, tn):
    B, S, D = x.shape
    n = w_main.shape[1]
    return pl.pallas_call(
        _inproj_kernel,
        grid=(B, S // tm, n // tn),
        in_specs=[pl.BlockSpec((None, tm, D), lambda b, i, j: (b, i, 0)),
                  pl.BlockSpec((1, D), lambda b, i, j: (0, 0)),
                  pl.BlockSpec((None, 1, D), lambda b, i, j: (b, 0, 0)),
                  pl.BlockSpec((None, 1, D), lambda b, i, j: (b, 0, 0)),
                  pl.BlockSpec((D, tn), lambda b, i, j: (0, j)),
                  pl.BlockSpec((D, 128), lambda b, i, j: (0, 0)),
                  pl.BlockSpec((1, 128), lambda b, i, j: (0, 0))],
        out_specs=[pl.BlockSpec((None, tm, tn), lambda b, i, j: (b, i, j)),
                   pl.BlockSpec((None, tm, 128), lambda b, i, j: (b, i, 0))],
        out_shape=[jax.ShapeDtypeStruct((B, S, n), BF16),
                   jax.ShapeDtypeStruct((B, S, 128), F32)],
        scratch_shapes=[pltpu.VMEM((tm, D), BF16)],
        compiler_params=_cparams(("parallel", "parallel", "arbitrary")),
        name="inproj",
    )(x, gain, scale, shift, w_main, w_if, b_if)


def _mlstm_kernel(q_ref, k_ref, v_ref, mo_ref, cwq_ref, cwk_ref, cbq_ref, cbk_ref, gi_ref, gf_ref, gain_ref,
                  y_ref, ct_scr, n_scr, m_scr, qtail, ktail, xext, *, L):
    @pl.when(pl.program_id(2) == 0)
    def _():
        ct_scr[...] = jnp.zeros_like(ct_scr)
        n_scr[...] = jnp.zeros_like(n_scr)
        m_scr[...] = jnp.zeros_like(m_scr)
        qtail[...] = jnp.zeros_like(qtail)
        ktail[...] = jnp.zeros_like(ktail)

    def conv_silu(x_ref, tail, cw_ref, cb_ref):
        x = x_ref[...].astype(F32)
        xext[0:8, :] = tail[...]
        xext[8:, :] = x
        acc = jnp.broadcast_to(cb_ref[...], x.shape)
        for j in range(CONV_WIDTH):
            acc = acc + cw_ref[j:j + 1, :] * xext[5 + j:5 + j + L, :]
        tail[...] = x[L - 8:, :]
        return acc * jax.nn.sigmoid(acc)

    q = conv_silu(q_ref, qtail, cwq_ref, cbq_ref)
    k = conv_silu(k_ref, ktail, cwk_ref, cbk_ref) * (M_HEAD_DIM ** -0.5)
    v = v_ref[...]
    qb = q.astype(BF16)
    kb = k.astype(BF16)

    li_row = gi_ref[...]
    fp = gf_ref[...]
    lf_row = jnp.minimum(fp, 0.0) - jnp.log1p(jnp.exp(-jnp.abs(fp)))

    ti = lax.broadcasted_iota(jnp.int32, (L, L), 0)
    si = lax.broadcasted_iota(jnp.int32, (L, L), 1)
    causal = si <= ti
    tril = causal.astype(BF16)
    eye = (si == ti).astype(BF16)

    def bf16_pieces(x):
        hi = x.astype(BF16).astype(F32)
        mid = (x - hi).astype(BF16).astype(F32)
        lo = (x - hi - mid).astype(BF16).astype(F32)
        return hi, mid, lo

    pieces = bf16_pieces(lf_row) + bf16_pieces(li_row)
    rid = lax.broadcasted_iota(jnp.int32, (8, L), 0)
    rows = jnp.zeros((8, L), F32)
    for idx, piece in enumerate(pieces):
        rows = jnp.where(rid == idx, piece, rows)
    rows = rows.astype(BF16)
    cum_rows = lax.dot_general(rows, tril, NT_DIMS, preferred_element_type=F32)
    cum_cols = lax.dot_general(tril, rows, NT_DIMS, preferred_element_type=F32)
    raw_cols = lax.dot_general(eye, rows, NT_DIMS, preferred_element_type=F32)
    b_row = cum_rows[0:1, :] + cum_rows[1:2, :] + cum_rows[2:3, :]
    b_col = cum_cols[:, 0:1] + cum_cols[:, 1:2] + cum_cols[:, 2:3]
    li_col = raw_cols[:, 3:4] + raw_cols[:, 4:5] + raw_cols[:, 5:6]

    m_prev = m_scr[0:1, 0:1]
    a_col = b_col + m_prev
    dmat = jnp.where(causal, b_col - b_row + li_row, -jnp.inf)
    m_t = jnp.maximum(a_col, jnp.max(dmat, axis=1, keepdims=True))
    p = jnp.exp(dmat - m_t)
    s = lax.dot_general(qb, kb, NT_DIMS, preferred_element_type=F32) * p
    w_inter = jnp.exp(a_col - m_t)
    ct = ct_scr[...]
    num = (jnp.dot(s.astype(BF16), v, preferred_element_type=F32)
           + w_inter * jnp.dot(qb, ct.astype(BF16), preferred_element_type=F32))
    n_row = n_scr[...]
    den = jnp.sum(s, axis=1, keepdims=True) + w_inter * jnp.sum(q * n_row, axis=1, keepdims=True)
    h = num / jnp.maximum(jnp.abs(den), jnp.exp(-m_t))

    m_new = m_t[L - 1:L, :]
    b_last = b_col[L - 1:L, :]
    g_prev = jnp.exp(b_last + m_prev - m_new)
    gs_col = jnp.exp(b_last - b_col + li_col - m_new)
    gv = (gs_col * v.astype(F32)).astype(BF16)
    ct_scr[...] = g_prev * ct + lax.dot_general(kb, gv, TN_DIMS, preferred_element_type=F32)
    n_scr[...] = g_prev * n_row + jnp.sum(gs_col * k, axis=0, keepdims=True)
    m_scr[...] = jnp.broadcast_to(m_new, m_scr.shape)

    y = _rms(h, gain_ref[...]) * jax.nn.sigmoid(mo_ref[...].astype(F32))
    y_ref[...] = y.astype(y_ref.dtype)


def _mlstm(proj, gates_rows, conv_w, conv_b, head_gain, L):
    B, S, _ = proj.shape
    H, Dh = M_HEADS, M_HEAD_DIM
    col = lambda off: (lambda b, h, c: (b, c, off + h))
    return pl.pallas_call(
        functools.partial(_mlstm_kernel, L=L),
        grid=(B, H, S // L),
        in_specs=[pl.BlockSpec((None, L, Dh), col(0)),
                  pl.BlockSpec((None, L, Dh), col(H)),
                  pl.BlockSpec((None, L, Dh), col(2 * H)),
                  pl.BlockSpec((None, L, Dh), col(3 * H)),
                  pl.BlockSpec((CONV_WIDTH, Dh), lambda b, h, c: (0, h)),
                  pl.BlockSpec((CONV_WIDTH, Dh), lambda b, h, c: (0, H + h)),
                  pl.BlockSpec((1, Dh), lambda b, h, c: (0, h)),
                  pl.BlockSpec((1, Dh), lambda b, h, c: (0, H + h)),
                  pl.BlockSpec((None, None, 1, L), lambda b, h, c: (b, h, 0, c)),
                  pl.BlockSpec((None, None, 1, L), lambda b, h, c: (b, H + h, 0, c)),
                  pl.BlockSpec((1, Dh), lambda b, h, c: (0, h))],
        out_specs=pl.BlockSpec((None, L, Dh), lambda b, h, c: (b, c, h)),
        out_shape=jax.ShapeDtypeStruct((B, S, H * Dh), BF16),
        scratch_shapes=[pltpu.VMEM((Dh, Dh), F32), pltpu.VMEM((1, Dh), F32), pltpu.VMEM((8, 128), F32),
                        pltpu.VMEM((8, Dh), F32), pltpu.VMEM((8, Dh), F32), pltpu.VMEM((L + 8, Dh), F32)],
        compiler_params=_cparams(("parallel", "parallel", "arbitrary")),
        name="mlstm",
    )(proj, proj, proj, proj, conv_w, conv_w, conv_b, conv_b, gates_rows, gates_rows, head_gain)


def _attn_kernel(q_ref, k_ref, v_ref, lq1_ref, lk1_ref, lq2_ref, lk2_ref, gain_ref, o_ref,
                 m_scr, l_scr, acc_scr, s_a, s_b, *, tq, tk):
    assert tq == tk
    qi = pl.program_id(2)
    c2 = (D_QK_DIM ** -0.5) * math.log2(math.e)

    m_scr[...] = jnp.full_like(m_scr, -jnp.inf)
    l_scr[...] = jnp.zeros_like(l_scr)
    acc_scr[...] = jnp.zeros_like(acc_scr)

    def scores(dst, k0, diagonal):
        for c in range(2):
            q = q_ref[:, c * D_QK_DIM:(c + 1) * D_QK_DIM]
            k = k_ref[pl.ds(k0, tk), c * D_QK_DIM:(c + 1) * D_QK_DIM]
            s = lax.dot_general(q, k, NT_DIMS, preferred_element_type=F32)
            if diagonal:
                tch = lax.broadcasted_iota(jnp.int32, s.shape, 0) // CHUNK
                sch = lax.broadcasted_iota(jnp.int32, s.shape, 1) // CHUNK
                s = jnp.where(sch <= tch, s, -jnp.inf)
            dst[c] = s

    def fold(src, k0):
        v = v_ref[pl.ds(k0, tk), :]
        for c in range(2):
            s = src[c]
            m_old = m_scr[c]
            m_new = jnp.maximum(m_old, jnp.max(s, axis=1, keepdims=True))
            alpha = jnp.exp2((m_old - m_new) * c2)
            p = jnp.exp2((s - jnp.concatenate([m_new] * (tk // 128), axis=1)) * c2)
            p_lanes = p[:, 0:128]
            for g in range(1, tk // 128):
                p_lanes = p_lanes + p[:, g * 128:(g + 1) * 128]
            l_scr[c] = alpha * l_scr[c] + p_lanes
            acc_scr[c] = (jnp.concatenate([alpha] * (D_V_DIM // 128), axis=1) * acc_scr[c]
                          + jnp.dot(p.astype(BF16), v, preferred_element_type=F32))
            m_scr[c] = m_new

    n_full = qi
    q0 = pl.multiple_of(qi * tq, tq)
    scores(s_a, 0, False)

    def two_chunks(m, carry):
        k0 = pl.multiple_of(2 * m * tk, tk)
        scores(s_b, k0 + tk, False)
        fold(s_a, k0)
        scores(s_a, k0 + 2 * tk, False)
        fold(s_b, k0 + tk)
        return carry

    lax.fori_loop(0, n_full // 2, two_chunks, 0)

    @pl.when(n_full % 2 == 1)
    def _():
        scores(s_b, q0, True)
        fold(s_a, q0 - tk)
        fold(s_b, q0)

    @pl.when(n_full % 2 == 0)
    def _():
        scores(s_b, q0, True)
        fold(s_b, q0)

    lam = (jnp.exp(jnp.sum(lq1_ref[...] * lk1_ref[...], axis=1, keepdims=True))
           - jnp.exp(jnp.sum(lq2_ref[...] * lk2_ref[...], axis=1, keepdims=True)) + LAM_INIT)
    l0 = jnp.sum(l_scr[0], axis=1, keepdims=True)
    l1 = jnp.sum(l_scr[1], axis=1, keepdims=True)
    o = acc_scr[0] / l0 - lam * (acc_scr[1] / l1)
    o_ref[...] = (_rms(o, gain_ref[...]) * (1.0 - LAM_INIT)).astype(o_ref.dtype)


def _attn(proj, lam_q1, lam_k1, lam_q2, lam_k2, head_gain, tq, tk):
    B, S, _ = proj.shape
    H = D_HEADS
    blk0 = 4 * M_HEADS
    lam_spec = pl.BlockSpec((1, D_QK_DIM), lambda b, h, qi: (0, 0))
    return pl.pallas_call(
        functools.partial(_attn_kernel, tq=tq, tk=tk),
        grid=(B, H, S // tq),
        in_specs=[pl.BlockSpec((None, tq, 256), lambda b, h, qi: (b, qi, blk0 + h)),
                  pl.BlockSpec((None, S, 256), lambda b, h, qi: (b, 0, blk0 + H + h)),
                  pl.BlockSpec((None, S, 256), lambda b, h, qi: (b, 0, blk0 + 2 * H + h)),
                  lam_spec, lam_spec, lam_spec, lam_spec,
                  pl.BlockSpec((1, D_V_DIM), lambda b, h, qi: (0, h))],
        out_specs=pl.BlockSpec((None, tq, D_V_DIM), lambda b, h, qi: (b, qi, h)),
        out_shape=jax.ShapeDtypeStruct((B, S, H * D_V_DIM), BF16),
        scratch_shapes=[pltpu.VMEM((2, tq, 128), F32), pltpu.VMEM((2, tq, 128), F32),
                        pltpu.VMEM((2, tq, D_V_DIM), F32),
                        pltpu.VMEM((2, tq, tk), F32), pltpu.VMEM((2, tq, tk), F32)],
        compiler_params=_cparams(("parallel", "parallel", "arbitrary")),
        name="diff_attn",
    )(proj, proj, proj, lam_q1, lam_k1, lam_q2, lam_k2, head_gain)


def _post_kernel(x_ref, ym_ref, yd_ref, gm_ref, gd_ref, gate1_ref, sc2_ref, sh2_ref, gpost_ref, gpre_ref,
                 wbm_ref, wbd_ref, wo_ref, wq_ref, x1_ref, h2_ref, qry_ref):
    bm = jnp.dot(ym_ref[...], wbm_ref[...], preferred_element_type=F32)
    bd = jnp.dot(yd_ref[...], wbd_ref[...], preferred_element_type=F32)
    merged = jax.nn.sigmoid(gm_ref[...].astype(F32)) * bm + jax.nn.sigmoid(gd_ref[...].astype(F32)) * bd
    y = jnp.dot(merged.astype(BF16), wo_ref[...], preferred_element_type=F32)
    x1 = x_ref[...] + gate1_ref[...] * _rms(y, gpost_ref[...])
    x1_ref[...] = x1
    h2 = (_rms(x1, gpre_ref[...]) * (1.0 + sc2_ref[...]) + sh2_ref[...]).astype(BF16)
    h2_ref[...] = h2
    qry_ref[...] = jnp.dot(h2, wq_ref[...], preferred_element_type=F32).astype(qry_ref.dtype)


def _post(x, y_m, y_d, proj, gate1, scale2, shift2, g_post, g_pre, w_br_m, w_br_d, w_out, w_query, tm):
    B, S, D = x.shape
    nq = w_query.shape[1]
    row = pl.BlockSpec((None, tm, D), lambda b, i: (b, i, 0))
    per_b = pl.BlockSpec((None, 1, D), lambda b, i: (b, 0, 0))
    vec = pl.BlockSpec((1, D), lambda b, i: (0, 0))
    wsq = pl.BlockSpec((D, D), lambda b, i: (0, 0))
    return pl.pallas_call(
        _post_kernel,
        grid=(B, S // tm),
        in_specs=[row, row, row,
                  pl.BlockSpec((None, tm, D), lambda b, i: (b, i, 7)),
                  pl.BlockSpec((None, tm, D), lambda b, i: (b, i, 8)),
                  per_b, per_b, per_b, vec, vec, wsq, wsq, wsq,
                  pl.BlockSpec((D, nq), lambda b, i: (0, 0))],
        out_specs=[row, row, pl.BlockSpec((None, tm, nq), lambda b, i: (b, i, 0))],
        out_shape=[jax.ShapeDtypeStruct((B, S, D), F32),
                   jax.ShapeDtypeStruct((B, S, D), BF16),
                   jax.ShapeDtypeStruct((B, S, nq), BF16)],
        compiler_params=_cparams(("parallel", "parallel")),
        name="post_mix",
    )(x, y_m, y_d, proj, proj, gate1, scale2, shift2, g_post, g_pre, w_br_m, w_br_d, w_out, w_query)


SUBLANES = 8


def _sort16_pairs():
    n, out, p = P_TOPK, [], 1
    while p < n:
        k = p
        while k >= 1:
            for j in range(k % p, n - k, 2 * k):
                for i in range(min(k, n - j - k)):
                    if (i + j) // (2 * p) == (i + j + k) // (2 * p):
                        out.append((i + j, i + j + k))
            k //= 2
        p *= 2
    return out


def _exchange(vs, i, j):
    hi, lo = jnp.maximum(vs[i], vs[j]), jnp.minimum(vs[i], vs[j])
    vs[i], vs[j] = hi, lo


def _merge_over_sublanes(vs):
    for shift in (4, 2, 1):
        other = [pltpu.roll(v, shift, axis=0) for v in vs]
        vs = [jnp.maximum(vs[k], other[P_TOPK - 1 - k]) for k in range(P_TOPK)]
        d = P_TOPK // 2
        while d >= 1:
            for k in range(P_TOPK):
                if k & d == 0:
                    _exchange(vs, k, k + d)
            d //= 2
    return vs


def _top16_of_keys(slices):
    vs = list(slices)
    for i, j in _sort16_pairs():
        _exchange(vs, i, j)
    return _merge_over_sublanes(vs)


def _route_kernel(q_ref, keys_ref, rank_ref, cnt_ref, ea_ref, eb_ref, *, tb_out):
    n_groups = N_KEYS // SUBLANES
    raw = []
    for c in range(2):
        q = q_ref[:, c * P_HALF:(c + 1) * P_HALF].astype(keys_ref.dtype)
        sc = lax.dot_general(keys_ref[c], q, NT_DIMS, preferred_element_type=F32)
        raw.append([sc[g * SUBLANES:(g + 1) * SUBLANES, :] for g in range(n_groups)])
    a_rows, b_rows = raw
    a_top = _top16_of_keys(a_rows)
    b_top = _top16_of_keys(b_rows)

    sub = lax.broadcasted_iota(jnp.int32, a_top[0].shape, 0)
    b_lo, b_hi = b_top[0], b_top[SUBLANES]
    for qq in range(1, SUBLANES):
        b_lo = jnp.where(sub == qq, b_top[qq], b_lo)
        b_hi = jnp.where(sub == qq, b_top[SUBLANES + qq], b_hi)
    cand = []
    for p in range(P_TOPK):
        n_valid = P_TOPK // (p + 1)
        cp = a_top[p] + b_lo
        cand.append(cp if n_valid >= SUBLANES else jnp.where(sub < n_valid, cp, -jnp.inf))
    extra = a_top[0] + b_hi
    for k in range(P_TOPK):
        cand[k], extra = jnp.maximum(cand[k], extra), jnp.minimum(cand[k], extra)
    best = _merge_over_sublanes(cand)

    tau = best[P_TOPK - 1]
    z = jnp.ones_like(tau)
    for k in range(1, P_TOPK):
        z = z + jnp.exp(best[k] - best[0])
    half_inv_z = 0.5 / z

    def store(ref, row0, val):
        for s in range(ref.shape[0]):
            ref[s, pl.ds(row0, val.shape[0]), :] = val[:, s * tb_out:(s + 1) * tb_out].astype(ref.dtype)

    ranks, weights = [], []
    for g in range(n_groups):
        count = jnp.zeros_like(tau)
        for qq in range(P_TOPK):
            count = jnp.where(a_rows[g] + b_top[qq] >= tau, float(qq + 1), count)
        rank = jnp.full_like(tau, float(P_TOPK))
        for qq in reversed(range(P_TOPK)):
            rank = jnp.where(b_top[qq] <= b_rows[g], float(qq), rank)
        store(cnt_ref, g * SUBLANES, count)
        store(ea_ref, g * SUBLANES, jnp.exp(a_rows[g] - a_top[0]))
        ranks.append(rank)
        weights.append(jnp.exp(b_rows[g] - b_top[0]) * half_inv_z)
        if g % 2 == 1:
            store(rank_ref, (g - 1) * SUBLANES, jnp.concatenate(ranks[-2:], axis=0))
            store(eb_ref, (g - 1) * SUBLANES, jnp.concatenate(weights[-2:], axis=0))


def _route(qry, keys, tb, tb_out):
    T = qry.shape[0]
    out_spec = pl.BlockSpec((tb // tb_out, None, N_KEYS, tb_out), lambda i, h: (i, h, 0, 0))
    shape = (T // tb_out, P_HEADS, N_KEYS, tb_out)
    return pl.pallas_call(
        functools.partial(_route_kernel, tb_out=tb_out),
        grid=(T // tb, P_HEADS),
        in_specs=[pl.BlockSpec((tb, 2 * P_HALF), lambda i, h: (i, h)),
                  pl.BlockSpec((None, 2, N_KEYS, P_HALF), lambda i, h: (h, 0, 0, 0))],
        out_specs=[out_spec] * 4,
        out_shape=[jax.ShapeDtypeStruct(shape, BF16), jax.ShapeDtypeStruct(shape, F32),
                   jax.ShapeDtypeStruct(shape, F32), jax.ShapeDtypeStruct(shape, BF16)],
        compiler_params=_cparams(("parallel", "parallel")),
        name="peer_route",
    )(qry, keys)


def _peer_kernel(ht_ref, u_ref, u_next_ref, vt_ref, rank_ref, cnt_ref, ea_ref, eb_ref, x1_ref, gate2_ref, gpost_ref,
                 o_ref,
                 act_a, act_b, acc_scr, skew_scr, *, n_groups, rows_per_group):
    j = pl.program_id(1)
    group = rows_per_group * N_KEYS
    tb = acc_scr.shape[1]

    @pl.when(j == 0)
    def _():
        acc_scr[...] = jnp.zeros_like(acc_scr)
        for h in range(P_HEADS):
            skew_scr[2 * h, 0:N_KEYS, 0:tb] = rank_ref[h]
            skew_scr[2 * h + 1, 0:N_KEYS, 0:tb] = eb_ref[h]

    def pre_activations(p):
        rows = pl.ds(pl.multiple_of(p * group, group), group)
        return jnp.dot(u_ref[rows, :], ht_ref[...], preferred_element_type=F32)

    def gated(act_ref, p):
        parts = []
        for g in range(rows_per_group):
            key_row = (j * n_groups + p) * rows_per_group + g
            gate = None
            for h in range(P_HEADS):
                count_row = cnt_ref[h, pl.ds(key_row, 1), :].astype(BF16)
                ea_row = ea_ref[h, pl.ds(key_row, 1), :].astype(BF16)
                picked = skew_scr[2 * h, 0:N_KEYS, 0:tb] < count_row
                term = ea_row * jnp.where(picked, skew_scr[2 * h + 1, 0:N_KEYS, 0:tb], jnp.zeros((), BF16))
                gate = term if gate is None else gate + term
            a = act_ref[g * N_KEYS:(g + 1) * N_KEYS, :]
            gelu2 = a * (1.0 + lax.erf(a * (2.0 ** -0.5)))
            parts.append(gelu2.astype(BF16) * gate)
        return jnp.concatenate(parts, axis=0)

    def accumulate(p, w):
        cols = pl.ds(pl.multiple_of(p * group, group), group)
        acc_scr[...] += jnp.dot(vt_ref[:, cols], w, preferred_element_type=F32)

    @pl.when(j == 0)
    def _():
        act_a[...] = pre_activations(0)

    def two_groups(p0, next_group):
        act_b[...] = pre_activations(p0 + 1)
        accumulate(p0, gated(act_a, p0))
        act_a[...] = next_group()
        accumulate(p0 + 1, gated(act_b, p0 + 1))

    def inner_trip(m, carry):
        two_groups(2 * m, lambda: pre_activations(2 * m + 2))
        return carry

    lax.fori_loop(0, n_groups // 2 - 1, inner_trip, 0)
    two_groups(n_groups - 2, lambda: jnp.dot(u_next_ref[...], ht_ref[...], preferred_element_type=F32))

    @pl.when(j == pl.num_programs(1) - 1)
    def _():
        o_ref[...] = x1_ref[...] + gate2_ref[...] * _rms(acc_scr[...].T, gpost_ref[...])


def _peer(h2, u, v, rank_t, cnt_t, ea_t, eb_t, x1, gate2, g_post, seq_len, tb, eb, rows_per_group):
    T, D = h2.shape
    E = u.shape[0]
    blocks_per_seq = seq_len // tb
    group = rows_per_group * N_KEYS
    h2_t = jnp.transpose(h2.reshape(T // tb, tb, D), (0, 2, 1))
    v_t = jnp.transpose(v.reshape(E // eb, eb, D), (0, 2, 1))
    route_spec = pl.BlockSpec((None, P_HEADS, N_KEYS, tb), lambda i, j: (i, 0, 0, 0))
    return pl.pallas_call(
        functools.partial(_peer_kernel, n_groups=eb // group, rows_per_group=rows_per_group),
        grid=(T // tb, E // eb),
        in_specs=[pl.BlockSpec((None, D, tb), lambda i, j: (i, 0, 0)),
                  pl.BlockSpec((eb, D), lambda i, j: (j, 0)),
                  pl.BlockSpec((group, D), lambda i, j: (((j + 1) % (E // eb)) * (eb // group), 0)),
                  pl.BlockSpec((None, D, eb), lambda i, j: (j, 0, 0)),
                  route_spec, route_spec, route_spec, route_spec,
                  pl.BlockSpec((tb, D), lambda i, j: (i, 0)),
                  pl.BlockSpec((None, 1, D), lambda i, j: (i // blocks_per_seq, 0, 0)),
                  pl.BlockSpec((1, D), lambda i, j: (0, 0))],
        out_specs=pl.BlockSpec((tb, D), lambda i, j: (i, 0)),
        out_shape=jax.ShapeDtypeStruct((T, D), F32),
        scratch_shapes=[pltpu.VMEM((group, tb), F32), pltpu.VMEM((group, tb), F32), pltpu.VMEM((D, tb), F32),
                        pltpu.VMEM((2 * P_HEADS, N_KEYS + 2 * SUBLANES, tb + 128), BF16)],
        compiler_params=_cparams(("parallel", "arbitrary")),
        name="peer_dense",
    )(h2_t, u, u, v_t, rank_t, cnt_t, ea_t, eb_t, x1, gate2, g_post)


def _pick(n, prefs):
    for p in prefs:
        if n % p == 0:
            return p
    raise ValueError(f"no supported tile for extent {n}")


def kernel(x, c, w_ada, b_ada, g_pre_mix, g_post_mix, g_pre_ffn, g_post_ffn, w_in, b_if, conv_w, conv_b,
           m_head_gain, lam_q1, lam_k1, lam_q2, lam_k2, d_head_gain, w_br_m, w_br_d, w_out,
           w_query, sub_keys, expert_u, expert_v):
    B, S, D = x.shape
    depth = w_ada.shape[0]
    assert depth == 1 and B <= 8
    m_width = M_HEADS * M_HEAD_DIM
    gate_off = 4 * m_width
    n_gate = 2 * M_HEADS

    for l in range(depth):
        c8 = jnp.pad(c, ((0, 8 - B), (0, 0)))
        ada = _ada(c8, w_ada[l], b_ada[l][None, :])[:B]
        shift1, scale1, gate1, shift2, scale2, gate2 = [a[:, None, :] for a in jnp.split(ada, ADA_PARTS, axis=-1)]

        w_in_l = w_in[l]
        w_main = jnp.concatenate([w_in_l[:, :gate_off], w_in_l[:, gate_off + n_gate:]], axis=1).astype(BF16)
        w_if = jnp.pad(w_in_l[:, gate_off:gate_off + n_gate], ((0, 0), (0, 128 - n_gate))).astype(BF16)
        b_if_p = jnp.pad(b_if[l], (0, 128 - n_gate))[None, :]

        proj, gates = _inproj(x, g_pre_mix[l][None, :], scale1, shift1, w_main, w_if, b_if_p,
                              tm=_pick(S, (1024, 512, 256)), tn=_pick(w_main.shape[1], (2304, 1152, 1024)))
        gates_rows = jnp.transpose(gates[:, :, :n_gate], (0, 2, 1))[:, :, None, :]

        y_m = _mlstm(proj, gates_rows, conv_w[l], conv_b[l][None, :], m_head_gain[l][None, :],
                     L=_pick(S, (256,)))
        y_d = _attn(proj, lam_q1[l][None, :], lam_k1[l][None, :], lam_q2[l][None, :], lam_k2[l][None, :],
                    d_head_gain[l][None, :], tq=_pick(S, (512, 256)), tk=_pick(S, (512, 256)))

        x1, h2, qry = _post(x, y_m, y_d, proj, gate1, scale2, shift2, g_post_mix[l][None, :],
                            g_pre_ffn[l][None, :], w_br_m[l].astype(BF16), w_br_d[l].astype(BF16),
                            w_out[l].astype(BF16), w_query[l].astype(BF16), tm=_pick(S, (512, 256)))

        T = B * S
        tb_peer = _pick(S, (512, 256))
        rank_t, cnt_t, ea_t, eb_t = _route(qry.reshape(T, -1), sub_keys[l].astype(BF16),
                                         tb=_pick(T, (2 * tb_peer,)), tb_out=tb_peer)
        x = _peer(h2.reshape(T, D), expert_u[l].astype(BF16), expert_v[l].astype(BF16), rank_t, cnt_t, ea_t, eb_t,
                  x1.reshape(T, D), gate2, g_post_ffn[l][None, :], seq_len=S,
                  tb=tb_peer, eb=2048, rows_per_group=2).reshape(B, S, D)
    return x
```

```python
import functools
import math

import jax
import jax.numpy as jnp
from jax import lax
from jax.experimental import pallas as pl
from jax.experimental.pallas import tpu as pltpu

F32 = jnp.float32
BF16 = jnp.bfloat16
HIGHEST = lax.Precision.HIGHEST

EPS = 1e-6
ADA_PARTS = 6
CHUNK = 64
M_HEADS = 4
M_HEAD_DIM = 256
CONV_WIDTH = 4
D_HEADS = 4
D_QK_DIM = 128
D_V_DIM = 256
P_HEADS = 8
N_KEYS = 128
P_TOPK = 16
P_HALF = 128
LAM_INIT = 0.8 - 0.6 * math.exp(-0.3 * 0)

V7X_VMEM_LIMIT_BYTES = 56 * 1024 * 1024

NT_DIMS = (((1,), (1,)), ((), ()))
TN_DIMS = (((0,), (0,)), ((), ()))


def _cparams(semantics):
    return pltpu.CompilerParams(dimension_semantics=semantics, vmem_limit_bytes=V7X_VMEM_LIMIT_BYTES)


def _rms(x, gain):
    return x * lax.rsqrt(jnp.mean(x * x, axis=-1, keepdims=True) + EPS) * gain


def _ada_kernel(c_ref, w_ref, b_ref, o_ref):
    c = c_ref[...]
    sc = c * jax.nn.sigmoid(c)
    o_ref[...] = jnp.dot(sc.astype(BF16), w_ref[...].astype(BF16), preferred_element_type=F32) + b_ref[...]


def _ada(c8, w_ada, b_ada):
    d = c8.shape[1]
    n = w_ada.shape[1]
    tn = 1024
    return pl.pallas_call(
        _ada_kernel,
        grid=(n // tn,),
        in_specs=[pl.BlockSpec((8, d), lambda j: (0, 0)),
                  pl.BlockSpec((d, tn), lambda j: (0, j)),
                  pl.BlockSpec((1, tn), lambda j: (0, j))],
        out_specs=pl.BlockSpec((8, tn), lambda j: (0, j)),
        out_shape=jax.ShapeDtypeStruct((8, n), F32),
        compiler_params=_cparams(("arbitrary",)),
        name="ada",
    )(c8, w_ada, b_ada)


def _inproj_kernel(x_ref, g_ref, sc_ref, sh_ref, w_ref, wif_ref, bif_ref, o_ref, oif_ref, h_scr):
    @pl.when(pl.program_id(2) == 0)
    def _():
        h = _rms(x_ref[...], g_ref[...]) * (1.0 + sc_ref[...]) + sh_ref[...]
        hb = h.astype(BF16)
        h_scr[...] = hb
        oif_ref[...] = jnp.dot(hb, wif_ref[...], preferred_element_type=F32) + bif_ref[...]

    res = jnp.dot(h_scr[...], w_ref[...], preferred_element_type=F32).astype(o_ref.dtype)
    for cb in range(o_ref.shape[0]):
        o_ref[cb] = res[:, cb * PROJ_COLS:(cb + 1) * PROJ_COLS]


PROJ_COLS = 256


def _inproj(x, gain, scale, shift, w_main, w_if, b_if, tm, tn):
    B, S, D = x.shape
    n = w_main.shape[1]
    cbs = tn // PROJ_COLS
    return pl.pallas_call(
        _inproj_kernel,
        grid=(B, S // tm, n // tn),
        in_specs=[pl.BlockSpec((None, tm, D), lambda b, i, j: (b, i, 0)),
                  pl.BlockSpec((1, D), lambda b, i, j: (0, 0)),
                  pl.BlockSpec((None, 1, D), lambda b, i, j: (b, 0, 0)),
                  pl.BlockSpec((None, 1, D), lambda b, i, j: (b, 0, 0)),
                  pl.BlockSpec((D, tn), lambda b, i, j: (0, j)),
                  pl.BlockSpec((D, 128), lambda b, i, j: (0, 0)),
                  pl.BlockSpec((1, 128), lambda b, i, j: (0, 0))],
        out_specs=[pl.BlockSpec((None, cbs, tm, PROJ_COLS), lambda b, i, j: (b, j, i, 0)),
                   pl.BlockSpec((None, tm, 128), lambda b, i, j: (b, i, 0))],
        out_shape=[jax.ShapeDtypeStruct((B, n // PROJ_COLS, S, PROJ_COLS), BF16),
                   jax.ShapeDtypeStruct((B, S, 128), F32)],
        scratch_shapes=[pltpu.VMEM((tm, D), BF16)],
        compiler_params=_cparams(("parallel", "parallel", "arbitrary")),
        name="inproj",
    )(x, gain, scale, shift, w_main, w_if, b_if)


def _mlstm_kernel(q_ref, k_ref, v_ref, mo_ref, cwq_ref, cwk_ref, cbq_ref, cbk_ref, gi_ref, gf_ref, gain_ref,
                  y_ref, ct_scr, n_scr, m_scr, qtail, ktail, xext, *, L):
    @pl.when(pl.program_id(2) == 0)
    def _():
        ct_scr[...] = jnp.zeros_like(ct_scr)
        n_scr[...] = jnp.zeros_like(n_scr)
        m_scr[...] = jnp.zeros_like(m_scr)
        qtail[...] = jnp.zeros_like(qtail)
        ktail[...] = jnp.zeros_like(ktail)

    def conv_silu(x_ref, tail, cw_ref, cb_ref):
        x = x_ref[...].astype(F32)
        xext[0:8, :] = tail[...]
        xext[8:, :] = x
        acc = jnp.broadcast_to(cb_ref[...], x.shape)
        for j in range(CONV_WIDTH):
            acc = acc + cw_ref[j:j + 1, :] * xext[5 + j:5 + j + L, :]
        tail[...] = x[L - 8:, :]
        return acc * jax.nn.sigmoid(acc)

    q = conv_silu(q_ref, qtail, cwq_ref, cbq_ref)
    k = conv_silu(k_ref, ktail, cwk_ref, cbk_ref) * (M_HEAD_DIM ** -0.5)
    v = v_ref[...]
    qb = q.astype(BF16)
    kb = k.astype(BF16)

    li_row = gi_ref[...]
    fp = gf_ref[...]
    lf_row = jnp.minimum(fp, 0.0) - jnp.log1p(jnp.exp(-jnp.abs(fp)))

    ti = lax.broadcasted_iota(jnp.int32, (L, L), 0)
    si = lax.broadcasted_iota(jnp.int32, (L, L), 1)
    causal = si <= ti
    tril = causal.astype(BF16)
    eye = (si == ti).astype(BF16)

    def bf16_pieces(x):
        hi = x.astype(BF16).astype(F32)
        mid = (x - hi).astype(BF16).astype(F32)
        lo = (x - hi - mid).astype(BF16).astype(F32)
        return hi, mid, lo

    pieces = bf16_pieces(lf_row) + bf16_pieces(li_row)
    rid = lax.broadcasted_iota(jnp.int32, (8, L), 0)
    rows = jnp.zeros((8, L), F32)
    for idx, piece in enumerate(pieces):
        rows = jnp.where(rid == idx, piece, rows)
    rows = rows.astype(BF16)
    cum_rows = lax.dot_general(rows, tril, NT_DIMS, preferred_element_type=F32)
    cum_cols = lax.dot_general(tril, rows, NT_DIMS, preferred_element_type=F32)
    raw_cols = lax.dot_general(eye, rows, NT_DIMS, preferred_element_type=F32)
    b_row = cum_rows[0:1, :] + cum_rows[1:2, :] + cum_rows[2:3, :]
    b_col = cum_cols[:, 0:1] + cum_cols[:, 1:2] + cum_cols[:, 2:3]
    li_col = raw_cols[:, 3:4] + raw_cols[:, 4:5] + raw_cols[:, 5:6]

    m_prev = m_scr[0:1, 0:1]
    a_col = b_col + m_prev
    dmat = jnp.where(causal, b_col - b_row + li_row, -jnp.inf)
    m_t = jnp.maximum(a_col, jnp.max(dmat, axis=1, keepdims=True))
    p = jnp.exp(dmat - m_t)
    s = lax.dot_general(qb, kb, NT_DIMS, preferred_element_type=F32) * p
    w_inter = jnp.exp(a_col - m_t)
    ct = ct_scr[...]
    num = (jnp.dot(s.astype(BF16), v, preferred_element_type=F32)
           + w_inter * jnp.dot(qb, ct.astype(BF16), preferred_element_type=F32))
    n_row = n_scr[...]
    den = jnp.sum(s, axis=1, keepdims=True) + w_inter * jnp.sum(q * n_row, axis=1, keepdims=True)
    h = num / jnp.maximum(jnp.abs(den), jnp.exp(-m_t))

    m_new = m_t[L - 1:L, :]
    b_last = b_col[L - 1:L, :]
    g_prev = jnp.exp(b_last + m_prev - m_new)
    gs_col = jnp.exp(b_last - b_col + li_col - m_new)
    gv = (gs_col * v.astype(F32)).astype(BF16)
    ct_scr[...] = g_prev * ct + lax.dot_general(kb, gv, TN_DIMS, preferred_element_type=F32)
    n_scr[...] = g_prev * n_row + jnp.sum(gs_col * k, axis=0, keepdims=True)
    m_scr[...] = jnp.broadcast_to(m_new, m_scr.shape)

    y = _rms(h, gain_ref[...]) * jax.nn.sigmoid(mo_ref[...].astype(F32))
    y_ref[...] = y.astype(y_ref.dtype)


def _mlstm(proj, gates_rows, conv_w, conv_b, head_gain, L):
    B, _, S, _ = proj.shape
    H, Dh = M_HEADS, M_HEAD_DIM
    assert Dh == PROJ_COLS
    col = lambda off: (lambda b, h, c: (b, off + h, c, 0))
    return pl.pallas_call(
        functools.partial(_mlstm_kernel, L=L),
        grid=(B, H, S // L),
        in_specs=[pl.BlockSpec((None, None, L, Dh), col(0)),
                  pl.BlockSpec((None, None, L, Dh), col(H)),
                  pl.BlockSpec((None, None, L, Dh), col(2 * H)),
                  pl.BlockSpec((None, None, L, Dh), col(3 * H)),
                  pl.BlockSpec((CONV_WIDTH, Dh), lambda b, h, c: (0, h)),
                  pl.BlockSpec((CONV_WIDTH, Dh), lambda b, h, c: (0, H + h)),
                  pl.BlockSpec((1, Dh), lambda b, h, c: (0, h)),
                  pl.BlockSpec((1, Dh), lambda b, h, c: (0, H + h)),
                  pl.BlockSpec((None, None, 1, L), lambda b, h, c: (b, h, 0, c)),
                  pl.BlockSpec((None, None, 1, L), lambda b, h, c: (b, H + h, 0, c)),
                  pl.BlockSpec((1, Dh), lambda b, h, c: (0, h))],
        out_specs=pl.BlockSpec((None, L, Dh), lambda b, h, c: (b, c, h)),
        out_shape=jax.ShapeDtypeStruct((B, S, H * Dh), BF16),
        scratch_shapes=[pltpu.VMEM((Dh, Dh), F32), pltpu.VMEM((1, Dh), F32), pltpu.VMEM((8, 128), F32),
                        pltpu.VMEM((8, Dh), F32), pltpu.VMEM((8, Dh), F32), pltpu.VMEM((L + 8, Dh), F32)],
        compiler_params=_cparams(("parallel", "parallel", "arbitrary")),
        name="mlstm",
    )(proj, proj, proj, proj, conv_w, conv_w, conv_b, conv_b, gates_rows, gates_rows, head_gain)


def _attn_kernel(q_ref, k_ref, v_ref, lq1_ref, lk1_ref, lq2_ref, lk2_ref, gain_ref, o_ref,
                 m_scr, l_scr, acc_scr, s_a, s_b, *, tq, tk):
    assert tq == tk
    qi = pl.program_id(2)
    c2 = (D_QK_DIM ** -0.5) * math.log2(math.e)

    m_scr[...] = jnp.full_like(m_scr, -jnp.inf)
    l_scr[...] = jnp.zeros_like(l_scr)
    acc_scr[...] = jnp.zeros_like(acc_scr)

    def scores(dst, k0, diagonal):
        for c in range(2):
            q = q_ref[:, c * D_QK_DIM:(c + 1) * D_QK_DIM]
            k = k_ref[pl.ds(k0, tk), c * D_QK_DIM:(c + 1) * D_QK_DIM]
            s = lax.dot_general(q, k, NT_DIMS, preferred_element_type=F32)
            if diagonal:
                tch = lax.broadcasted_iota(jnp.int32, s.shape, 0) // CHUNK
                sch = lax.broadcasted_iota(jnp.int32, s.shape, 1) // CHUNK
                s = jnp.where(sch <= tch, s, -jnp.inf)
            dst[c] = s

    def fold(src, k0):
        v = v_ref[pl.ds(k0, tk), :]
        for c in range(2):
            s = src[c]
            m_old = m_scr[c]
            m_new = jnp.maximum(m_old, jnp.max(s, axis=1, keepdims=True))
            alpha = jnp.exp2((m_old - m_new) * c2)
            p = jnp.exp2((s - jnp.concatenate([m_new] * (tk // 128), axis=1)) * c2)
            p_lanes = p[:, 0:128]
            for g in range(1, tk // 128):
                p_lanes = p_lanes + p[:, g * 128:(g + 1) * 128]
            l_scr[c] = alpha * l_scr[c] + p_lanes
            acc_scr[c] = (jnp.concatenate([alpha] * (D_V_DIM // 128), axis=1) * acc_scr[c]
                          + jnp.dot(p.astype(BF16), v, preferred_element_type=F32))
            m_scr[c] = m_new

    n_full = qi
    q0 = pl.multiple_of(qi * tq, tq)
    scores(s_a, 0, False)

    def two_chunks(m, carry):
        k0 = pl.multiple_of(2 * m * tk, tk)
        scores(s_b, k0 + tk, False)
        fold(s_a, k0)
        scores(s_a, k0 + 2 * tk, False)
        fold(s_b, k0 + tk)
        return carry

    lax.fori_loop(0, n_full // 2, two_chunks, 0)

    @pl.when(n_full % 2 == 1)
    def _():
        scores(s_b, q0, True)
        fold(s_a, q0 - tk)
        fold(s_b, q0)

    @pl.when(n_full % 2 == 0)
    def _():
        scores(s_b, q0, True)
        fold(s_b, q0)

    lam = (jnp.exp(jnp.sum(lq1_ref[...] * lk1_ref[...], axis=1, keepdims=True))
           - jnp.exp(jnp.sum(lq2_ref[...] * lk2_ref[...], axis=1, keepdims=True)) + LAM_INIT)
    l0 = jnp.sum(l_scr[0], axis=1, keepdims=True)
    l1 = jnp.sum(l_scr[1], axis=1, keepdims=True)
    o = acc_scr[0] / l0 - lam * (acc_scr[1] / l1)
    o_ref[...] = (_rms(o, gain_ref[...]) * (1.0 - LAM_INIT)).astype(o_ref.dtype)


def _attn(proj, lam_q1, lam_k1, lam_q2, lam_k2, head_gain, tq, tk):
    B, _, S, _ = proj.shape
    H = D_HEADS
    assert 2 * D_QK_DIM == PROJ_COLS and D_V_DIM == PROJ_COLS
    blk0 = 4 * M_HEADS
    lam_spec = pl.BlockSpec((1, D_QK_DIM), lambda b, h, qi: (0, 0))
    return pl.pallas_call(
        functools.partial(_attn_kernel, tq=tq, tk=tk),
        grid=(B, H, S // tq),
        in_specs=[pl.BlockSpec((None, None, tq, PROJ_COLS), lambda b, h, qi: (b, blk0 + h, qi, 0)),
                  pl.BlockSpec((None, None, S, PROJ_COLS), lambda b, h, qi: (b, blk0 + H + h, 0, 0)),
                  pl.BlockSpec((None, None, S, PROJ_COLS), lambda b, h, qi: (b, blk0 + 2 * H + h, 0, 0)),
                  lam_spec, lam_spec, lam_spec, lam_spec,
                  pl.BlockSpec((1, D_V_DIM), lambda b, h, qi: (0, h))],
        out_specs=pl.BlockSpec((None, tq, D_V_DIM), lambda b, h, qi: (b, qi, h)),
        out_shape=jax.ShapeDtypeStruct((B, S, H * D_V_DIM), BF16),
        scratch_shapes=[pltpu.VMEM((2, tq, 128), F32), pltpu.VMEM((2, tq, 128), F32),
                        pltpu.VMEM((2, tq, D_V_DIM), F32),
                        pltpu.VMEM((2, tq, tk), F32), pltpu.VMEM((2, tq, tk), F32)],
        compiler_params=_cparams(("parallel", "parallel", "arbitrary")),
        name="diff_attn",
    )(proj, proj, proj, lam_q1, lam_k1, lam_q2, lam_k2, head_gain)


def _post_kernel(x_ref, ym_ref, yd_ref, gm_ref, gd_ref, gate1_ref, sc2_ref, sh2_ref, gpost_ref, gpre_ref,
                 wbm_ref, wbd_ref, wo_ref, wq_ref, x1_ref, h2_ref, qry_ref):
    bm = jnp.dot(ym_ref[...], wbm_ref[...], preferred_element_type=F32)
    bd = jnp.dot(yd_ref[...], wbd_ref[...], preferred_element_type=F32)
    gm = jnp.concatenate([gm_ref[cb] for cb in range(gm_ref.shape[0])], axis=1).astype(F32)
    gd = jnp.concatenate([gd_ref[cb] for cb in range(gd_ref.shape[0])], axis=1).astype(F32)
    merged = jax.nn.sigmoid(gm) * bm + jax.nn.sigmoid(gd) * bd
    y = jnp.dot(merged.astype(BF16), wo_ref[...], preferred_element_type=F32)
    x1 = x_ref[...] + gate1_ref[...] * _rms(y, gpost_ref[...])
    x1_ref[...] = x1
    h2 = (_rms(x1, gpre_ref[...]) * (1.0 + sc2_ref[...]) + sh2_ref[...]).astype(BF16)
    h2_ref[...] = h2
    qry_ref[...] = jnp.dot(h2, wq_ref[...], preferred_element_type=F32).astype(qry_ref.dtype)


def _post(x, y_m, y_d, proj, gate1, scale2, shift2, g_post, g_pre, w_br_m, w_br_d, w_out, w_query, tm):
    B, S, D = x.shape
    nq = w_query.shape[1]
    gcb = D // PROJ_COLS
    row = pl.BlockSpec((None, tm, D), lambda b, i: (b, i, 0))
    per_b = pl.BlockSpec((None, 1, D), lambda b, i: (b, 0, 0))
    vec = pl.BlockSpec((1, D), lambda b, i: (0, 0))
    wsq = pl.BlockSpec((D, D), lambda b, i: (0, 0))
    return pl.pallas_call(
        _post_kernel,
        grid=(B, S // tm),
        in_specs=[row, row, row,
                  pl.BlockSpec((None, gcb, tm, PROJ_COLS), lambda b, i: (b, 7, i, 0)),
                  pl.BlockSpec((None, gcb, tm, PROJ_COLS), lambda b, i: (b, 8, i, 0)),
                  per_b, per_b, per_b, vec, vec, wsq, wsq, wsq,
                  pl.BlockSpec((D, nq), lambda b, i: (0, 0))],
        out_specs=[row, row, pl.BlockSpec((None, tm, nq), lambda b, i: (b, i, 0))],
        out_shape=[jax.ShapeDtypeStruct((B, S, D), F32),
                   jax.ShapeDtypeStruct((B, S, D), BF16),
                   jax.ShapeDtypeStruct((B, S, nq), BF16)],
        compiler_params=_cparams(("parallel", "parallel")),
        name="post_mix",
    )(x, y_m, y_d, proj, proj, gate1, scale2, shift2, g_post, g_pre, w_br_m, w_br_d, w_out, w_query)


SUBLANES = 8


def _sort16_pairs():
    n, out, p = P_TOPK, [], 1
    while p < n:
        k = p
        while k >= 1:
            for j in range(k % p, n - k, 2 * k):
                for i in range(min(k, n - j - k)):
                    if (i + j) // (2 * p) == (i + j + k) // (2 * p):
                        out.append((i + j, i + j + k))
            k //= 2
        p *= 2
    return out


def _exchange(vs, i, j):
    hi, lo = jnp.maximum(vs[i], vs[j]), jnp.minimum(vs[i], vs[j])
    vs[i], vs[j] = hi, lo


def _merge_over_sublanes(vs):
    for shift in (4, 2, 1):
        other = [pltpu.roll(v, shift, axis=0) for v in vs]
        vs = [jnp.maximum(vs[k], other[P_TOPK - 1 - k]) for k in range(P_TOPK)]
        d = P_TOPK // 2
        while d >= 1:
            for k in range(P_TOPK):
                if k & d == 0:
                    _exchange(vs, k, k + d)
            d //= 2
    return vs


def _top16_of_keys(slices):
    vs = list(slices)
    for i, j in _sort16_pairs():
        _exchange(vs, i, j)
    return _merge_over_sublanes(vs)


def _route_kernel(q_ref, keys_ref, rank_ref, cnt_ref, ea_ref, eb_ref, *, tb_out):
    n_groups = N_KEYS // SUBLANES
    raw = []
    for c in range(2):
        q = q_ref[:, c * P_HALF:(c + 1) * P_HALF].astype(keys_ref.dtype)
        sc = lax.dot_general(keys_ref[c], q, NT_DIMS, preferred_element_type=F32)
        raw.append([sc[g * SUBLANES:(g + 1) * SUBLANES, :] for g in range(n_groups)])
    a_rows, b_rows = raw
    a_top = _top16_of_keys(a_rows)
    b_top = _top16_of_keys(b_rows)

    sub = lax.broadcasted_iota(jnp.int32, a_top[0].shape, 0)
    b_lo, b_hi = b_top[0], b_top[SUBLANES]
    for qq in range(1, SUBLANES):
        b_lo = jnp.where(sub == qq, b_top[qq], b_lo)
        b_hi = jnp.where(sub == qq, b_top[SUBLANES + qq], b_hi)
    cand = []
    for p in range(P_TOPK):
        n_valid = P_TOPK // (p + 1)
        cp = a_top[p] + b_lo
        cand.append(cp if n_valid >= SUBLANES else jnp.where(sub < n_valid, cp, -jnp.inf))
    extra = a_top[0] + b_hi
    for k in range(P_TOPK):
        cand[k], extra = jnp.maximum(cand[k], extra), jnp.minimum(cand[k], extra)
    best = _merge_over_sublanes(cand)

    tau = best[P_TOPK - 1]
    z = jnp.ones_like(tau)
    for k in range(1, P_TOPK):
        z = z + jnp.exp(best[k] - best[0])
    half_inv_z = 0.5 / z

    def store(ref, row0, val):
        for s in range(ref.shape[0]):
            ref[s, pl.ds(row0, val.shape[0]), :] = val[:, s * tb_out:(s + 1) * tb_out].astype(ref.dtype)

    ranks, weights = [], []
    for g in range(n_groups):
        count = jnp.zeros_like(tau)
        for qq in range(P_TOPK):
            count = jnp.where(a_rows[g] + b_top[qq] >= tau, float(qq + 1), count)
        rank = jnp.full_like(tau, float(P_TOPK))
        for qq in reversed(range(P_TOPK)):
            rank = jnp.where(b_top[qq] <= b_rows[g], float(qq), rank)
        store(cnt_ref, g * SUBLANES, count)
        store(ea_ref, g * SUBLANES, jnp.exp(a_rows[g] - a_top[0]))
        ranks.append(rank)
        weights.append(jnp.exp(b_rows[g] - b_top[0]) * half_inv_z)
        if g % 2 == 1:
            store(rank_ref, (g - 1) * SUBLANES, jnp.concatenate(ranks[-2:], axis=0))
            store(eb_ref, (g - 1) * SUBLANES, jnp.concatenate(weights[-2:], axis=0))


def _route(qry, keys, tb, tb_out):
    T = qry.shape[0]
    out_spec = pl.BlockSpec((tb // tb_out, None, N_KEYS, tb_out), lambda i, h: (i, h, 0, 0))
    shape = (T // tb_out, P_HEADS, N_KEYS, tb_out)
    return pl.pallas_call(
        functools.partial(_route_kernel, tb_out=tb_out),
        grid=(T // tb, P_HEADS),
        in_specs=[pl.BlockSpec((tb, 2 * P_HALF), lambda i, h: (i, h)),
                  pl.BlockSpec((None, 2, N_KEYS, P_HALF), lambda i, h: (h, 0, 0, 0))],
        out_specs=[out_spec] * 4,
        out_shape=[jax.ShapeDtypeStruct(shape, BF16), jax.ShapeDtypeStruct(shape, F32),
                   jax.ShapeDtypeStruct(shape, F32), jax.ShapeDtypeStruct(shape, BF16)],
        compiler_params=_cparams(("parallel", "parallel")),
        name="peer_route",
    )(qry, keys)


def _peer_kernel(ht_ref, u_ref, u_next_ref, vt_ref, rank_ref, cnt_ref, ea_ref, eb_ref, x1_ref, gate2_ref, gpost_ref,
                 o_ref,
                 act_a, act_b, acc_scr, skew_scr, *, n_groups, rows_per_group):
    j = pl.program_id(1)
    group = rows_per_group * N_KEYS
    tb = acc_scr.shape[1]

    @pl.when(j == 0)
    def _():
        acc_scr[...] = jnp.zeros_like(acc_scr)
        for h in range(P_HEADS):
            skew_scr[2 * h, 0:N_KEYS, 0:tb] = rank_ref[h]
            skew_scr[2 * h + 1, 0:N_KEYS, 0:tb] = eb_ref[h]

    def pre_activations(p):
        rows = pl.ds(pl.multiple_of(p * group, group), group)
        return jnp.dot(u_ref[rows, :], ht_ref[...], preferred_element_type=F32)

    def gated(act_ref, p):
        parts = []
        for g in range(rows_per_group):
            key_row = (j * n_groups + p) * rows_per_group + g
            gate = None
            for h in range(P_HEADS):
                count_row = cnt_ref[h, pl.ds(key_row, 1), :].astype(BF16)
                ea_row = ea_ref[h, pl.ds(key_row, 1), :].astype(BF16)
                picked = skew_scr[2 * h, 0:N_KEYS, 0:tb] < count_row
                term = ea_row * jnp.where(picked, skew_scr[2 * h + 1, 0:N_KEYS, 0:tb], jnp.zeros((), BF16))
                gate = term if gate is None else gate + term
            a = act_ref[g * N_KEYS:(g + 1) * N_KEYS, :]
            gelu2 = a * (1.0 + lax.erf(a * (2.0 ** -0.5)))
            parts.append(gelu2.astype(BF16) * gate)
        return jnp.concatenate(parts, axis=0)

    def accumulate(p, w):
        cols = pl.ds(pl.multiple_of(p * group, group), group)
        acc_scr[...] += jnp.dot(vt_ref[:, cols], w, preferred_element_type=F32)

    @pl.when(j == 0)
    def _():
        act_a[...] = pre_activations(0)

    def two_groups(p0, next_group):
        act_b[...] = pre_activations(p0 + 1)
        accumulate(p0, gated(act_a, p0))
        act_a[...] = next_group()
        accumulate(p0 + 1, gated(act_b, p0 + 1))

    def inner_trip(m, carry):
        two_groups(2 * m, lambda: pre_activations(2 * m + 2))
        return carry

    lax.fori_loop(0, n_groups // 2 - 1, inner_trip, 0)
    two_groups(n_groups - 2, lambda: jnp.dot(u_next_ref[...], ht_ref[...], preferred_element_type=F32))

    @pl.when(j == pl.num_programs(1) - 1)
    def _():
        o_ref[...] = x1_ref[...] + gate2_ref[...] * _rms(acc_scr[...].T, gpost_ref[...])


def _peer(h2, u, v, rank_t, cnt_t, ea_t, eb_t, x1, gate2, g_post, seq_len, tb, eb, rows_per_group):
    T, D = h2.shape
    E = u.shape[0]
    blocks_per_seq = seq_len // tb
    group = rows_per_group * N_KEYS
    h2_t = jnp.transpose(h2.reshape(T // tb, tb, D), (0, 2, 1))
    v_t = jnp.transpose(v.reshape(E // eb, eb, D), (0, 2, 1))
    route_spec = pl.BlockSpec((None, P_HEADS, N_KEYS, tb), lambda i, j: (i, 0, 0, 0))
    return pl.pallas_call(
        functools.partial(_peer_kernel, n_groups=eb // group, rows_per_group=rows_per_group),
        grid=(T // tb, E // eb),
        in_specs=[pl.BlockSpec((None, D, tb), lambda i, j: (i, 0, 0)),
                  pl.BlockSpec((eb, D), lambda i, j: (j, 0)),
                  pl.BlockSpec((group, D), lambda i, j: (((j + 1) % (E // eb)) * (eb // group), 0)),
                  pl.BlockSpec((None, D, eb), lambda i, j: (j, 0, 0)),
                  route_spec, route_spec, route_spec, route_spec,
                  pl.BlockSpec((tb, D), lambda i, j: (i, 0)),
                  pl.BlockSpec((None, 1, D), lambda i, j: (i // blocks_per_seq, 0, 0)),
                  pl.BlockSpec((1, D), lambda i, j: (0, 0))],
        out_specs=pl.BlockSpec((tb, D), lambda i, j: (i, 0)),
        out_shape=jax.ShapeDtypeStruct((T, D), F32),
        scratch_shapes=[pltpu.VMEM((group, tb), F32), pltpu.VMEM((group, tb), F32), pltpu.VMEM((D, tb), F32),
                        pltpu.VMEM((2 * P_HEADS, N_KEYS + 2 * SUBLANES, tb + 128), BF16)],
        compiler_params=_cparams(("parallel", "arbitrary")),
        name="peer_dense",
    )(h2_t, u, u, v_t, rank_t, cnt_t, ea_t, eb_t, x1, gate2, g_post)


def _pick(n, prefs):
    for p in prefs:
        if n % p == 0:
            return p
    raise ValueError(f"no supported tile for extent {n}")


def kernel(x, c, w_ada, b_ada, g_pre_mix, g_post_mix, g_pre_ffn, g_post_ffn, w_in, b_if, conv_w, conv_b,
           m_head_gain, lam_q1, lam_k1, lam_q2, lam_k2, d_head_gain, w_br_m, w_br_d, w_out,
           w_query, sub_keys, expert_u, expert_v):
    B, S, D = x.shape
    depth = w_ada.shape[0]
    assert depth == 1 and B <= 8
    m_width = M_HEADS * M_HEAD_DIM
    gate_off = 4 * m_width
    n_gate = 2 * M_HEADS

    for l in range(depth):
        c8 = jnp.pad(c, ((0, 8 - B), (0, 0)))
        ada = _ada(c8, w_ada[l], b_ada[l][None, :])[:B]
        shift1, scale1, gate1, shift2, scale2, gate2 = [a[:, None, :] for a in jnp.split(ada, ADA_PARTS, axis=-1)]

        w_in_l = w_in[l]
        w_main = jnp.concatenate([w_in_l[:, :gate_off], w_in_l[:, gate_off + n_gate:]], axis=1).astype(BF16)
        w_if = jnp.pad(w_in_l[:, gate_off:gate_off + n_gate], ((0, 0), (0, 128 - n_gate))).astype(BF16)
        b_if_p = jnp.pad(b_if[l], (0, 128 - n_gate))[None, :]

        proj, gates = _inproj(x, g_pre_mix[l][None, :], scale1, shift1, w_main, w_if, b_if_p,
                              tm=_pick(S, (1024, 512, 256)), tn=_pick(w_main.shape[1], (2304, 1152, 1024)))
        gates_rows = jnp.transpose(gates[:, :, :n_gate], (0, 2, 1))[:, :, None, :]

        y_m = _mlstm(proj, gates_rows, conv_w[l], conv_b[l][None, :], m_head_gain[l][None, :],
                     L=_pick(S, (256,)))
        y_d = _attn(proj, lam_q1[l][None, :], lam_k1[l][None, :], lam_q2[l][None, :], lam_k2[l][None, :],
                    d_head_gain[l][None, :], tq=_pick(S, (512, 256)), tk=_pick(S, (512, 256)))

        x1, h2, qry = _post(x, y_m, y_d, proj, gate1, scale2, shift2, g_post_mix[l][None, :],
                            g_pre_ffn[l][None, :], w_br_m[l].astype(BF16), w_br_d[l].astype(BF16),
                            w_out[l].astype(BF16), w_query[l].astype(BF16), tm=_pick(S, (512, 256)))

        T = B * S
        tb_peer = _pick(S, (512, 256))
        rank_t, cnt_t, ea_t, eb_t = _route(qry.reshape(T, -1), sub_keys[l].astype(BF16),
                                         tb=_pick(T, (2 * tb_peer,)), tb_out=tb_peer)
        x = _peer(h2.reshape(T, D), expert_u[l].astype(BF16), expert_v[l].astype(BF16), rank_t, cnt_t, ea_t, eb_t,
                  x1.reshape(T, D), gate2, g_post_ffn[l][None, :], seq_len=S,
                  tb=tb_peer, eb=2048, rows_per_group=2).reshape(B, S, D)
    return x
```
